```python
import jax, jax.numpy as jnp
from jax import lax
import numpy as np

D_MODEL = 2048
BATCH = 1
SEQ = 8192
DEPTH = 2
DEC_BATCH = 32
DEC_SEQ = 1
PAST_LEN = 8192
PAGE_SIZE = 128

N_A_LAYERS = DEPTH // 2
N_B_LAYERS = DEPTH - N_A_LAYERS
D_FF = 5632
D_RNN = D_MODEL
N_RG_BLOCKS = 16
RG_BLOCK = D_RNN // N_RG_BLOCKS
CONV_WIDTH = 4
RG_C = 8.0
HEAD_DIM = 128
N_HEADS = D_MODEL // HEAD_DIM
WINDOWS = (128, 512, 2048)
DILATIONS = (1, 4, 16)
N_GROUPS = 3
ROPE_DIM = HEAD_DIM // 4
ROPE_THETA = 500000.0
Q_BLOCK = 128
EPS = 1e-6
QKV_WIDTH = N_GROUPS * N_HEADS * HEAD_DIM

kernel_name = 'yoco_rglru_dilated_swa_macaron_step'


def rmsnorm(x, g):
    xf = x.astype(jnp.float32)
    var = jnp.mean(xf * xf, axis=-1, keepdims=True)
    return (xf * lax.rsqrt(var + EPS)).astype(x.dtype) * g


def ffn_half(x, g, w_up, w_down):
    gate, up = jnp.split(rmsnorm(x, g) @ w_up, 2, axis=-1)
    return x + 0.5 * ((jax.nn.silu(gate) * up) @ w_down)


def rope(x, pos):
    half = ROPE_DIM // 2
    inv = ROPE_THETA ** (-jnp.arange(half, dtype=jnp.float32) / half)
    ang = pos.astype(jnp.float32)[:, None] * inv[None, :]
    cos = jnp.cos(ang)[None, :, None, :]
    sin = jnp.sin(ang)[None, :, None, :]
    xr = x[..., :ROPE_DIM].astype(jnp.float32)
    x1, x2 = xr[..., :half], xr[..., half:]
    rot = jnp.concatenate([x1 * cos - x2 * sin, x2 * cos + x1 * sin], axis=-1).astype(x.dtype)
    return jnp.concatenate([rot, x[..., ROPE_DIM:]], axis=-1)


def _lin_combine(left, right):
    a1, b1 = left
    a2, b2 = right
    return a1 * a2, a2 * b1 + b2


def rglru_block(xn, h0, conv0, w_in, conv_w, conv_b, ga_w, ga_b, gx_w, gx_b, lam, w_out):
    B, S, _ = xn.shape
    f32 = jnp.float32
    gate_br, rec = jnp.split(xn @ w_in, 2, axis=-1)
    ext = jnp.concatenate([conv0.astype(rec.dtype), rec], axis=1)
    xc = conv_b + ext[:, 0:S] * conv_w[0]
    for j in range(1, CONV_WIDTH):
        xc = xc + ext[:, j:j + S] * conv_w[j]
    new_conv = ext[:, -(CONV_WIDTH - 1):]
    xb = xc.reshape(B, S, N_RG_BLOCKS, RG_BLOCK)
    r = jax.nn.sigmoid(jnp.einsum('bsnc,ncd->bsnd', xb, ga_w).reshape(B, S, D_RNN) + ga_b)
    i = jax.nn.sigmoid(jnp.einsum('bsnc,ncd->bsnd', xb, gx_w).reshape(B, S, D_RNN) + gx_b)
    log_a = -RG_C * r.astype(f32) * jax.nn.softplus(-lam.astype(f32))
    a = jnp.exp(log_a)
    b = jnp.sqrt(-jnp.expm1(2.0 * log_a)) * (i * xc).astype(f32)
    b = b.at[:, 0].add(a[:, 0] * h0.astype(f32))
    _, h = lax.associative_scan(_lin_combine, (a, b), axis=1)
    y = (h.astype(xn.dtype) * jax.nn.gelu(gate_br)) @ w_out
    return y, h[:, -1].astype(h0.dtype), new_conv.astype(conv0.dtype)


def shared_kv(x, g, w_kv, pos):
    B, S, _ = x.shape
    kv = (rmsnorm(x, g) @ w_kv).reshape(B, S, 2, N_GROUPS * N_HEADS, HEAD_DIM)
    k = rope(kv[:, :, 0], pos)
    kv = jnp.stack([k, kv[:, :, 1]], axis=2)
    return kv.reshape(B, S, 2, N_GROUPS, N_HEADS, HEAD_DIM)


def dilated_group(q, k_ext, v_ext, q_idx, dilation, n_keys):
    idx = q_idx[:, None] - dilation * jnp.arange(n_keys, dtype=jnp.int32)[None, :]
    valid = idx >= 0
    idx = jnp.maximum(idx, 0)
    kg = jnp.take(k_ext, idx, axis=1)
    vg = jnp.take(v_ext, idx, axis=1)
    s = jnp.einsum('bqhd,bqkhd->bqhk', q, kg).astype(jnp.float32)
    s = jnp.where(valid[None, :, None, :], s, -jnp.inf)
    m = jnp.max(s, axis=-1, keepdims=True)
    p = jnp.exp(s - m)
    den = jnp.sum(p, axis=-1, keepdims=True)
    o = jnp.einsum('bqhk,bqkhd->bqhd', (p / den).astype(v_ext.dtype), vg)
    lse = (m + jnp.log(den))[..., 0]
    return o, lse


def dilated_mixture_attention(q, k_exts, v_exts, offsets):
    B, S = q.shape[0], q.shape[1]
    qb = Q_BLOCK if S % Q_BLOCK == 0 else S
    nb = S // qb
    q_blocks = jnp.moveaxis(q.reshape(B, nb, qb, N_GROUPS, N_HEADS, HEAD_DIM), 1, 0)

    def block(args):
        b_i, q_blk = args
        local = b_i * qb + jnp.arange(qb, dtype=jnp.int32)
        outs, lses = [], []
        for g in range(N_GROUPS):
            o, l = dilated_group(q_blk[:, :, g], k_exts[g], v_exts[g], offsets[g] + local,
                                 DILATIONS[g], WINDOWS[g] // DILATIONS[g] + 1)
            outs.append(o)
            lses.append(l)
        w = jax.nn.softmax(jnp.stack(lses, axis=0), axis=0)
        out = outs[0] * w[0][..., None].astype(outs[0].dtype)
        for g in range(1, N_GROUPS):
            out = out + outs[g] * w[g][..., None].astype(outs[g].dtype)
        return out

    o = lax.map(block, (jnp.arange(nb, dtype=jnp.int32), q_blocks))
    return jnp.moveaxis(o, 0, 1).reshape(B, S, N_HEADS * HEAD_DIM)


def attn_queries(xn, w_q, pos):
    B, S, _ = xn.shape
    q = (xn @ w_q).reshape(B, S, N_GROUPS * N_HEADS, HEAD_DIM)
    q = rope(q, pos) * (HEAD_DIM ** -0.5)
    return q.reshape(B, S, N_GROUPS, N_HEADS, HEAD_DIM)


def setup_inputs(seed: int = 0) -> dict:
    key = jax.random.key(seed)
    ks = iter(jax.random.split(key, 40))
    f32 = jnp.float32

    def nrm(shape, scale):
        return jax.random.normal(next(ks), shape, f32) * scale

    def gain(shape):
        return 1.0 + nrm(shape, 0.02)

    lens = [min(w, PAST_LEN) for w in WINDOWS]
    a0 = jax.random.uniform(next(ks), (N_A_LAYERS, D_RNN), f32, 0.9, 0.999)
    s0 = a0 ** (1.0 / RG_C)
    lam = jnp.log(s0) - jnp.log1p(-s0)
    return {
        'x_prompt': nrm((BATCH, SEQ, D_MODEL), 1.0),
        'x_sample': nrm((DEC_BATCH, DEC_SEQ, D_MODEL), 1.0),
        'state_h': nrm((N_A_LAYERS, DEC_BATCH, D_RNN), 0.5),
        'state_conv': nrm((N_A_LAYERS, DEC_BATCH, CONV_WIDTH - 1, D_RNN), 1.0),
        'cache_kv_g1': nrm((DEC_BATCH, lens[0], 2, N_HEADS, HEAD_DIM), 1.0),
        'cache_kv_g2': nrm((DEC_BATCH, lens[1], 2, N_HEADS, HEAD_DIM), 1.0),
        'cache_kv_g3': nrm((DEC_BATCH, lens[2], 2, N_HEADS, HEAD_DIM), 1.0),
        'ffn1_norm': gain((DEPTH, D_MODEL)),
        'ffn1_w_up': nrm((DEPTH, D_MODEL, 2 * D_FF), D_MODEL ** -0.5),
        'ffn1_w_down': nrm((DEPTH, D_FF, D_MODEL), D_FF ** -0.5),
        'mix_norm': gain((DEPTH, D_MODEL)),
        'ffn2_norm': gain((DEPTH, D_MODEL)),
        'ffn2_w_up': nrm((DEPTH, D_MODEL, 2 * D_FF), D_MODEL ** -0.5),
        'ffn2_w_down': nrm((DEPTH, D_FF, D_MODEL), D_FF ** -0.5),
        'rg_w_in': nrm((N_A_LAYERS, D_MODEL, 2 * D_RNN), D_MODEL ** -0.5),
        'rg_conv_w': nrm((N_A_LAYERS, CONV_WIDTH, D_RNN), CONV_WIDTH ** -0.5),
        'rg_conv_b': nrm((N_A_LAYERS, D_RNN), 0.01),
        'rg_gate_a_w': nrm((N_A_LAYERS, N_RG_BLOCKS, RG_BLOCK, RG_BLOCK), RG_BLOCK ** -0.5),
        'rg_gate_a_b': nrm((N_A_LAYERS, D_RNN), 0.01),
        'rg_gate_x_w': nrm((N_A_LAYERS, N_RG_BLOCKS, RG_BLOCK, RG_BLOCK), RG_BLOCK ** -0.5),
        'rg_gate_x_b': nrm((N_A_LAYERS, D_RNN), 0.01),
        'rg_lambda': lam,
        'rg_w_out': nrm((N_A_LAYERS, D_RNN, D_MODEL), D_RNN ** -0.5),
        'kv_norm': gain((D_MODEL,)),
        'w_kv': nrm((D_MODEL, 2 * QKV_WIDTH), D_MODEL ** -0.5),
        'attn_w_q': nrm((N_B_LAYERS, D_MODEL, QKV_WIDTH), D_MODEL ** -0.5),
        'attn_w_o': nrm((N_B_LAYERS, N_HEADS * HEAD_DIM, D_MODEL), (N_HEADS * HEAD_DIM) ** -0.5),
        'final_norm': gain((D_MODEL,)),
    }


def reference(x_prompt, x_sample, state_h, state_conv, cache_kv_g1, cache_kv_g2, cache_kv_g3,
              ffn1_norm, ffn1_w_up, ffn1_w_down, mix_norm, ffn2_norm, ffn2_w_up, ffn2_w_down,
              rg_w_in, rg_conv_w, rg_conv_b, rg_gate_a_w, rg_gate_a_b, rg_gate_x_w, rg_gate_x_b,
              rg_lambda, rg_w_out, kv_norm, w_kv, attn_w_q, attn_w_o, final_norm):
    caches = [cache_kv_g1, cache_kv_g2, cache_kv_g3]
    xp, xs = x_prompt, x_sample
    Bp, Sp = xp.shape[0], xp.shape[1]
    pos_p = jnp.arange(Sp, dtype=jnp.int32)
    pos_s = PAST_LEN + jnp.arange(xs.shape[1], dtype=jnp.int32)
    p_h, p_conv, s_h, s_conv = [], [], [], []
    p_kv, s_kv = [], []
    kp_ext, vp_ext, ks_ext, vs_ext = [], [], [], []
    p_off = [0] * N_GROUPS
    s_off = [c.shape[1] for c in caches]
    for layer in range(DEPTH):
        if layer == N_A_LAYERS:
            kvp = shared_kv(xp, kv_norm, w_kv, pos_p)
            kvs = shared_kv(xs, kv_norm, w_kv, pos_s)
            for g in range(N_GROUPS):
                kp_ext.append(kvp[:, :, 0, g])
                vp_ext.append(kvp[:, :, 1, g])
                p_kv.append(kvp[:, -min(WINDOWS[g], Sp):, :, g])
                ext = jnp.concatenate([caches[g].astype(kvs.dtype), kvs[:, :, :, g]], axis=1)
                ks_ext.append(ext[:, :, 0])
                vs_ext.append(ext[:, :, 1])
                s_kv.append(ext[:, -caches[g].shape[1]:].astype(caches[g].dtype))
        xp = ffn_half(xp, ffn1_norm[layer], ffn1_w_up[layer], ffn1_w_down[layer])
        xs = ffn_half(xs, ffn1_norm[layer], ffn1_w_up[layer], ffn1_w_down[layer])
        if layer < N_A_LAYERS:
            l = layer
            rg = (rg_w_in[l], rg_conv_w[l], rg_conv_b[l], rg_gate_a_w[l], rg_gate_a_b[l],
                  rg_gate_x_w[l], rg_gate_x_b[l], rg_lambda[l], rg_w_out[l])
            yp, hp, cp = rglru_block(rmsnorm(xp, mix_norm[layer]), jnp.zeros((Bp, D_RNN), xp.dtype),
                                     jnp.zeros((Bp, CONV_WIDTH - 1, D_RNN), xp.dtype), *rg)
            ys, hs, cs = rglru_block(rmsnorm(xs, mix_norm[layer]), state_h[l], state_conv[l], *rg)
            xp = xp + yp
            xs = xs + ys
            p_h.append(hp)
            p_conv.append(cp)
            s_h.append(hs)
            s_conv.append(cs)
        else:
            j = layer - N_A_LAYERS
            qp = attn_queries(rmsnorm(xp, mix_norm[layer]), attn_w_q[j], pos_p)
            qs = attn_queries(rmsnorm(xs, mix_norm[layer]), attn_w_q[j], pos_s)
            xp = xp + dilated_mixture_attention(qp, kp_ext, vp_ext, p_off) @ attn_w_o[j]
            xs = xs + dilated_mixture_attention(qs, ks_ext, vs_ext, s_off) @ attn_w_o[j]
        xp = ffn_half(xp, ffn2_norm[layer], ffn2_w_up[layer], ffn2_w_down[layer])
        xs = ffn_half(xs, ffn2_norm[layer], ffn2_w_up[layer], ffn2_w_down[layer])
    y_prompt = rmsnorm(xp, final_norm)
    y_sample = rmsnorm(xs, final_norm)
    p_state_h = jnp.stack(p_h, axis=0)
    p_state_conv = jnp.stack(p_conv, axis=0)
    s_state_h = jnp.stack(s_h, axis=0)
    s_state_conv = jnp.stack(s_conv, axis=0)
    return (y_prompt, y_sample, p_state_h, p_state_conv, p_kv[0], p_kv[1], p_kv[2],
            s_state_h, s_state_conv, s_kv[0], s_kv[1], s_kv[2])
```

```python
import functools

import jax
import jax.numpy as jnp
from jax import lax
from jax.experimental import pallas as pl
from jax.experimental.pallas import tpu as pltpu

F32 = jnp.float32
BF16 = jnp.bfloat16

D_MODEL = 2048
D_FF = 5632
D_RNN = 2048
N_RG_BLOCKS = 16
RG_BLOCK = D_RNN // N_RG_BLOCKS
CONV_WIDTH = 4
RG_C = 8.0
HEAD_DIM = 128
N_HEADS = 16
N_GROUPS = 3
WINDOWS = (128, 512, 2048)
DILATIONS = (1, 4, 16)
N_KEYS = 129
ROPE_DIM = 32
ROPE_THETA = 500000.0
EPS = 1e-6
PAST_LEN = 8192
GW =N_HEADS * HEAD_DIM

LANES = 128
SUBLANES = 8
VMEM_LIMIT_BYTES = 56 * 1024 * 1024


def _params(*sem):
    return pltpu.CompilerParams(dimension_semantics=sem, vmem_limit_bytes=VMEM_LIMIT_BYTES)


def _rms(x, g):
    var = jnp.mean(x * x, axis=-1, keepdims=True)
    return (x * lax.rsqrt(var + EPS)) * g


def _softplus(z):
    return jnp.maximum(z, 0.0) + jnp.log1p(jnp.exp(-jnp.abs(z)))


def _ffn_kernel(x_ref, g_ref, wg_ref, wu_ref, wd_ref, fg_ref, o_ref, xn_ref, acc_ref, *, nf, final):
    j = pl.program_id(1)

    @pl.when(j == 0)
    def _():
        xn_ref[...] = _rms(x_ref[...], g_ref[...]).astype(BF16)
        acc_ref[...] = jnp.zeros_like(acc_ref)

    xn = xn_ref[...]
    gate = jnp.dot(xn, wg_ref[...], preferred_element_type=F32)
    up = jnp.dot(xn, wu_ref[...], preferred_element_type=F32)
    h = (gate * jax.nn.sigmoid(gate)) * up
    acc_ref[...] += jnp.dot(h.astype(BF16), wd_ref[...], preferred_element_type=F32)

    @pl.when(j == nf - 1)
    def _():
        y = x_ref[...] + 0.5 * acc_ref[...]
        if final:
            y = _rms(y, fg_ref[...])
        o_ref[...] = y


def _ffn(x, g, w_up, w_down, fg, *, final, tm, tf):
    m, d = x.shape
    f = w_down.shape[0]
    nf = f // tf
    return pl.pallas_call(
        functools.partial(_ffn_kernel, nf=nf, final=final),
        grid=(m // tm, nf),
        in_specs=[
            pl.BlockSpec((tm, d), lambda i, j: (i, 0)),
            pl.BlockSpec((1, d), lambda i, j: (0, 0)),
            pl.BlockSpec((d, tf), lambda i, j: (0, j)),
            pl.BlockSpec((d, tf), lambda i, j: (0, j + nf)),
            pl.BlockSpec((tf, d), lambda i, j: (j, 0)),
            pl.BlockSpec((1, d), lambda i, j: (0, 0)),
        ],
        out_specs=pl.BlockSpec((tm, d), lambda i, j: (i, 0)),
        out_shape=jax.ShapeDtypeStruct((m, d), F32),
        scratch_shapes=[pltpu.VMEM((tm, d), BF16), pltpu.VMEM((tm, d), F32)],
        compiler_params=_params("parallel", "arbitrary"),
        name="ffn",
    )(x, g, w_up, w_up, w_down, fg)


def _rg_gates(xc, gaw, gab, gxw, gxb, sp):
    xcb = xc.astype(BF16)
    r = jax.nn.sigmoid(jnp.dot(xcb, gaw, preferred_element_type=F32) + gab)
    ig = jax.nn.sigmoid(jnp.dot(xcb, gxw, preferred_element_type=F32) + gxb)
    log_a = (-RG_C * r) * sp
    a = jnp.exp(log_a)
    b = jnp.sqrt(-jnp.tanh(log_a) * (1.0 + a * a)) * (ig * xc)
    return a, b


def _rg_prompt_kernel(x_ref, g_ref, win_ref, cw_ref, cb_ref, gaw_ref, gab_ref, gxw_ref, gxb_ref,
                      lam_ref, wout_ref, o_ref, hlast_ref, conv_ref,
                      ext_ref, gate_ref, y_ref, h_ref, *, tm):
    i = pl.program_id(0)
    pad = SUBLANES

    @pl.when(i == 0)
    def _():
        ext_ref[0:pad, :] = jnp.zeros((pad, D_RNN), F32)
        h_ref[...] = jnp.zeros_like(h_ref)

    x = x_ref[...]
    xn = _rms(x, g_ref[...]).astype(BF16)
    gate_ref[...] = jnp.dot(xn, win_ref[:, 0:D_RNN], preferred_element_type=F32)
    ext_ref[pad:pad + tm, :] = jnp.dot(xn, win_ref[:, D_RNN:2 * D_RNN], preferred_element_type=F32)

    row = lax.broadcasted_iota(jnp.int32, (tm, RG_BLOCK), 0)
    for n in range(N_RG_BLOCKS):
        cs = slice(n * RG_BLOCK, (n + 1) * RG_BLOCK)
        xc = cb_ref[:, cs] + ext_ref[pad - 3:pad - 3 + tm, cs] * cw_ref[0:1, cs]
        for j in range(1, CONV_WIDTH):
            xc = xc + ext_ref[pad - 3 + j:pad - 3 + j + tm, cs] * cw_ref[j:j + 1, cs]
        sp = _softplus(-lam_ref[:, cs])
        a, b = _rg_gates(xc, gaw_ref[n], gab_ref[:, cs], gxw_ref[n], gxb_ref[:, cs], sp)
        s = 1
        while s < tm:
            a_sh = pltpu.roll(a, s, 0)
            b_sh = pltpu.roll(b, s, 0)
            keep = row >= s
            b = jnp.where(keep, a * b_sh + b, b)
            a = jnp.where(keep, a * a_sh, a)
            s *= 2
        h = a * h_ref[:, cs] + b
        h_ref[:, cs] = h[tm - 1:tm, :]
        y_ref[:, cs] = (h * jax.nn.gelu(gate_ref[:, cs])).astype(BF16)

    o_ref[...] = x + jnp.dot(y_ref[...], wout_ref[...], preferred_element_type=F32)
    hlast_ref[...] = h_ref[...]
    conv_ref[...] = ext_ref[pad + tm - 3:pad + tm, :]
    ext_ref[0:pad, :] = ext_ref[tm:tm + pad, :]


def _rg_prompt(x, g, w_in, cw, cb, gaw, gab, gxw, gxb, lam, w_out, *, tm):
    m, d = x.shape
    const2 = lambda i: (0, 0)
    const3 = lambda i: (0, 0, 0)
    one = pl.Buffered(1)
    return pl.pallas_call(
        functools.partial(_rg_prompt_kernel, tm=tm),
        grid=(m // tm,),
        in_specs=[
            pl.BlockSpec((tm, d), lambda i: (i, 0)),
            pl.BlockSpec((1, d), const2),
            pl.BlockSpec((d, 2 * D_RNN), const2, pipeline_mode=one),
            pl.BlockSpec((CONV_WIDTH, D_RNN), const2),
            pl.BlockSpec((1, D_RNN), const2),
            pl.BlockSpec((N_RG_BLOCKS, RG_BLOCK, RG_BLOCK), const3),
            pl.BlockSpec((1, D_RNN), const2),
            pl.BlockSpec((N_RG_BLOCKS, RG_BLOCK, RG_BLOCK), const3),
            pl.BlockSpec((1, D_RNN), const2),
            pl.BlockSpec((1, D_RNN), const2),
            pl.BlockSpec((D_RNN, d), const2, pipeline_mode=one),
        ],
        out_specs=[
            pl.BlockSpec((tm, d), lambda i: (i, 0)),
            pl.BlockSpec((1, D_RNN), const2),
            pl.BlockSpec((CONV_WIDTH - 1, D_RNN), const2),
        ],
        out_shape=[
            jax.ShapeDtypeStruct((m, d), F32),
            jax.ShapeDtypeStruct((1, D_RNN), F32),
            jax.ShapeDtypeStruct((CONV_WIDTH - 1, D_RNN), F32),
        ],
        scratch_shapes=[
            pltpu.VMEM((tm + SUBLANES, D_RNN), F32),
            pltpu.VMEM((tm, D_RNN), F32),
            pltpu.VMEM((tm, D_RNN), BF16),
            pltpu.VMEM((1, D_RNN), F32),
        ],
        compiler_params=_params("arbitrary"),
        name="rg_prompt",
    )(x, g, w_in, cw, cb, gaw, gab, gxw, gxb, lam, w_out)


def _rg_sample_kernel(x_ref, g_ref, win_ref, cw_ref, cb_ref, gaw_ref, gab_ref, gxw_ref, gxb_ref,
                      lam_ref, wout_ref, h0_ref, c0_ref, o_ref, hnew_ref, cnew_ref, y_ref):
    x = x_ref[...]
    xn = _rms(x, g_ref[...]).astype(BF16)
    gate = jnp.dot(xn, win_ref[:, 0:D_RNN], preferred_element_type=F32)
    rec = jnp.dot(xn, win_ref[:, D_RNN:2 * D_RNN], preferred_element_type=F32)
    for n in range(N_RG_BLOCKS):
        cs = slice(n * RG_BLOCK, (n + 1) * RG_BLOCK)
        taps = [c0_ref[:, j * D_RNN + n * RG_BLOCK:j * D_RNN + (n + 1) * RG_BLOCK]
                for j in range(CONV_WIDTH - 1)] + [rec[:, cs]]
        xc = cb_ref[:, cs] + taps[0] * cw_ref[0:1, cs]
        for j in range(1, CONV_WIDTH):
            xc = xc + taps[j] * cw_ref[j:j + 1, cs]
        sp = _softplus(-lam_ref[:, cs])
        a, b = _rg_gates(xc, gaw_ref[n], gab_ref[:, cs], gxw_ref[n], gxb_ref[:, cs], sp)
        h = a * h0_ref[:, cs] + b
        hnew_ref[:, cs] = h
        y_ref[:, cs] = (h * jax.nn.gelu(gate[:, cs])).astype(BF16)
    o_ref[...] = x + jnp.dot(y_ref[...], wout_ref[...], preferred_element_type=F32)
    cnew_ref[:, 0:2 * D_RNN] = c0_ref[:, D_RNN:3 * D_RNN]
    cnew_ref[:, 2 * D_RNN:3 * D_RNN] = rec


def _rg_sample(x, g, w_in, cw, cb, gaw, gab, gxw, gxb, lam, w_out, h0, c0):
    m, d = x.shape
    vm = pl.BlockSpec(memory_space=pltpu.VMEM)
    return pl.pallas_call(
        _rg_sample_kernel,
        in_specs=[vm] * 13,
        out_specs=[vm] * 3,
        out_shape=[
            jax.ShapeDtypeStruct((m, d), F32),
            jax.ShapeDtypeStruct((m, D_RNN), F32),
            jax.ShapeDtypeStruct((m, (CONV_WIDTH - 1) * D_RNN), F32),
        ],
        scratch_shapes=[pltpu.VMEM((m, D_RNN), BF16)],
        compiler_params=pltpu.CompilerParams(vmem_limit_bytes=VMEM_LIMIT_BYTES),
        name="rg_sample",
    )(x, g, w_in, cw, cb, gaw, gab, gxw, gxb, lam, w_out, h0, c0)


_PROJ_CHUNK = 512


def _proj_kernel(*refs, tm, dil, n_mats, rope, scale, pos0, pos_stride, i0, out_dtype, has_tail):
    x_ref, g_ref, inv_ref = refs[:3]
    w_refs = refs[3:3 + n_mats]
    o_ref = refs[3 + n_mats]
    tail_ref = refs[4 + n_mats] if has_tail else None
    slab_ref = refs[-1]
    i = pl.program_id(0)
    half = ROPE_DIM // 2

    xn = _rms(x_ref[...], g_ref[...]).astype(BF16)
    if any(rope):
        r = lax.broadcasted_iota(jnp.int32, (tm, LANES), 0)
        pos = pos0 + (i * tm + r) * pos_stride
        ang = pos.astype(F32) * inv_ref[...]
        lane = lax.broadcasted_iota(jnp.int32, (tm, LANES), 1)
        sin = jnp.sin(ang)
        cos = jnp.cos(ang)
        s_lo = jnp.where(lane < half, -sin, 0.0)
        s_hi = jnp.where((lane >= half) & (lane < ROPE_DIM), sin, 0.0)

    for k in range(n_mats):
        for c0 in range(0, GW, _PROJ_CHUNK):
            acc = jnp.dot(xn, w_refs[k][:, c0:c0 + _PROJ_CHUNK], preferred_element_type=F32)
            for c in range(c0, c0 + _PROJ_CHUNK, LANES):
                a = acc[:, c - c0:c - c0 + LANES]
                if rope[k]:
                    a = (a * cos + pltpu.roll(a, half, 1) * s_hi
                         + pltpu.roll(a, LANES - half, 1) * s_lo)
                if scale[k] != 1.0:
                    a = a * scale[k]
                if has_tail:
                    @pl.when(i >= i0)
                    def _(a=a, k=k, c=c):
                        tail_ref[k, :, c:c + LANES] = a
                if dil == 1:
                    o_ref[k, 0, :, c:c + LANES] = a.astype(out_dtype)
                else:
                    s = c // LANES % slab_ref.shape[0]
                    slab_ref[s] = a
                    for r_ in range(dil):
                        o_ref[k, r_, :, c:c + LANES] = (
                            slab_ref[s, pl.ds(r_, tm // dil, stride=dil), :].astype(out_dtype))


def _proj(x, g, inv, w, col_blocks, *, dil, rope, scale, pos0, pos_stride, tm, out_dtype, tail=0):
    m, d = x.shape
    n_mats = len(col_blocks)
    assert m % tm == 0 and tm % dil == 0
    lrows = m // dil
    tail_rows = max(tail, tm) if tail else 0
    i0 = (m - tail_rows) // tm if tail else 0
    const2 = lambda i: (0, 0)
    in_specs = [pl.BlockSpec((tm, d), lambda i: (i, 0)), pl.BlockSpec((1, d), const2),
                pl.BlockSpec((1, LANES), const2)]
    for cb in col_blocks:
        in_specs.append(pl.BlockSpec((d, GW), lambda i, cb=cb: (0, cb), pipeline_mode=pl.Buffered(1)))
    out_specs = [pl.BlockSpec((n_mats, dil, tm // dil, GW), lambda i: (0, 0, i, 0))]
    out_shape = [jax.ShapeDtypeStruct((n_mats, dil, lrows, GW), out_dtype)]
    if tail:
        out_specs.append(pl.BlockSpec((n_mats, tm, GW), lambda i: (0, jnp.maximum(i - i0, 0), 0)))
        out_shape.append(jax.ShapeDtypeStruct((n_mats, tail_rows, GW), F32))
    outs = pl.pallas_call(
        functools.partial(_proj_kernel, tm=tm, dil=dil, n_mats=n_mats, rope=tuple(rope),
                          scale=tuple(scale), pos0=pos0, pos_stride=pos_stride, i0=i0,
                          out_dtype=out_dtype, has_tail=bool(tail)),
        grid=(m // tm,),
        in_specs=in_specs,
        out_specs=out_specs,
        out_shape=out_shape,
        scratch_shapes=[pltpu.VMEM((2, tm, LANES), F32)],
        compiler_params=_params("arbitrary"),
        name="norm_proj",
    )(x, g, inv, *([w] * n_mats))
    if tail:
        return outs[0], outs[1][:, tail_rows - tail:]
    return outs[0]


def _attn_kernel(q_ref, kc_ref, kp_ref, vc_ref, vp_ref, o_ref, lse_ref, *, tq, nb):
    ub = pl.program_id(0) % nb
    blk = LANES
    nsub = tq // blk
    row = lax.broadcasted_iota(jnp.int32, (blk, blk), 0)
    col = lax.broadcasted_iota(jnp.int32, (blk, blk), 1)
    lane = lax.broadcasted_iota(jnp.int32, (blk, LANES), 1)
    cur_ok = col <= row
    prev_tri = col >= row
    neg = -jnp.inf
    dn = (((1,), (1,)), ((), ()))

    for c in range(nsub):
        rs = slice(c * blk, (c + 1) * blk)
        prev_ok = prev_tri if c > 0 else jnp.logical_and(prev_tri, ub > 0)

        def head(h, lse_tile, rs=rs, c=c, prev_ok=prev_ok):
            hs = pl.ds(pl.multiple_of(h * HEAD_DIM, HEAD_DIM), HEAD_DIM)
            q = q_ref[rs, hs]
            kc = kc_ref[rs, hs]
            vc = vc_ref[rs, hs]
            if c == 0:
                kp = kp_ref[:, hs]
                vp = vp_ref[:, hs]
            else:
                ps = slice((c - 1) * blk, c * blk)
                kp = kc_ref[ps, hs]
                vp = vc_ref[ps, hs]
            s_c = jnp.where(cur_ok, lax.dot_general(q, kc, dn, preferred_element_type=F32), neg)
            s_p = jnp.where(prev_ok, lax.dot_general(q, kp, dn, preferred_element_type=F32), neg)
            m = jnp.maximum(jnp.max(s_c, axis=1, keepdims=True), jnp.max(s_p, axis=1, keepdims=True))
            p_c = jnp.exp(s_c - m)
            p_p = jnp.exp(s_p - m)
            den = jnp.sum(p_c, axis=1, keepdims=True) + jnp.sum(p_p, axis=1, keepdims=True)
            o = (jnp.dot(p_c.astype(BF16), vc, preferred_element_type=F32)
                 + jnp.dot(p_p.astype(BF16), vp, preferred_element_type=F32))
            o_ref[rs, hs] = o / den
            return jnp.where(lane == h, m + jnp.log(den), lse_tile)

        lse_ref[rs, :] = lax.fori_loop(0, N_HEADS, head, jnp.zeros((blk, LANES), F32))


def _attn_group(q, kv, g, *, tq):
    _, dil, lrows, _ = q.shape
    s = dil * lrows
    nb = lrows // tq
    sub = tq // LANES
    qv = q.reshape(s, GW)
    kvv = kv.reshape(2, s, GW)
    prev_idx = lambda i: jnp.maximum(i * sub - 1, 0)
    o, lse = pl.pallas_call(
        functools.partial(_attn_kernel, tq=tq, nb=nb),
        grid=(s // tq,),
        in_specs=[
            pl.BlockSpec((tq, GW), lambda i: (i, 0)),
            pl.BlockSpec((None, tq, GW), lambda i: (0, i, 0)),
            pl.BlockSpec((None, LANES, GW), lambda i: (0, prev_idx(i), 0)),
            pl.BlockSpec((None, tq, GW), lambda i: (1, i, 0)),
            pl.BlockSpec((None, LANES, GW), lambda i: (1, prev_idx(i), 0)),
        ],
        out_specs=[
            pl.BlockSpec((tq, GW), lambda i: (i, 0)),
            pl.BlockSpec((tq, LANES), lambda i: (i, 0)),
        ],
        out_shape=[
            jax.ShapeDtypeStruct((s, GW), F32),
            jax.ShapeDtypeStruct((s, LANES), F32),
        ],
        compiler_params=_params("parallel"),
        name="attn_group",
    )(qv, kvv, kvv, kvv, kvv)
    return o.reshape(dil, lrows, GW), lse.reshape(dil, lrows, LANES)


def _merge_wo_kernel(x_ref, o1_ref, o2_ref, o3_ref, l1_ref, l2_ref, l3_ref, w_ref, out_ref,
                     onat_ref, lnat_ref, a_ref, *, tm):
    o_refs = (o1_ref, o2_ref, o3_ref)
    l_refs = (l1_ref, l2_ref, l3_ref)
    for g, dil in enumerate(DILATIONS):
        rows = tm // dil
        for r in range(dil):
            dst = pl.ds(r, rows, stride=dil) if dil > 1 else slice(None)
            lnat_ref[g, dst, :] = l_refs[g][r]
            for h in range(N_HEADS):
                onat_ref[g * N_HEADS + h, dst, :] = o_refs[g][r, :, h * HEAD_DIM:(h + 1) * HEAD_DIM]
    l1, l2, l3 = lnat_ref[0], lnat_ref[1], lnat_ref[2]
    mx = jnp.maximum(jnp.maximum(l1, l2), l3)
    e1, e2, e3 = jnp.exp(l1 - mx), jnp.exp(l2 - mx), jnp.exp(l3 - mx)
    tot = (e1 + e2) + e3
    w1, w2, w3 = e1 / tot, e2 / tot, e3 / tot
    for h in range(N_HEADS):
        a = onat_ref[h] * w1[:, h:h + 1]
        a = a + onat_ref[N_HEADS + h] * w2[:, h:h + 1]
        a = a + onat_ref[2 * N_HEADS + h] * w3[:, h:h + 1]
        a_ref[:, h * HEAD_DIM:(h + 1) * HEAD_DIM] = a.astype(BF16)
    out_ref[...] = x_ref[...] + jnp.dot(a_ref[...], w_ref[...], preferred_element_type=F32)


def _merge_wo(x, os_, ls_, w_o, *, tm):
    m, d = x.shape
    rowblk = lambda i: (i, 0)
    deint = lambda i: (0, i, 0)
    in_specs = [pl.BlockSpec((tm, d), rowblk)]
    in_specs += [pl.BlockSpec((dil, tm // dil, GW), deint) for dil in DILATIONS]
    in_specs += [pl.BlockSpec((dil, tm // dil, LANES), deint) for dil in DILATIONS]
    in_specs += [pl.BlockSpec((GW, d), lambda i: (0, 0), pipeline_mode=pl.Buffered(1))]
    return pl.pallas_call(
        functools.partial(_merge_wo_kernel, tm=tm),
        grid=(m // tm,),
        in_specs=in_specs,
        out_specs=pl.BlockSpec((tm, d), rowblk),
        out_shape=jax.ShapeDtypeStruct((m, d), F32),
        scratch_shapes=[pltpu.VMEM((N_GROUPS * N_HEADS, tm, LANES), F32),
                        pltpu.VMEM((N_GROUPS, tm, LANES), F32),
                        pltpu.VMEM((tm, GW), BF16)],
        compiler_params=_params("parallel"),
        name="merge_wo",
    )(x, *os_, *ls_, w_o)


def _bf16_round(v):
    return v.astype(BF16).astype(F32)


def _attn_sample_kernel(q_ref, kvn_ref, c1_ref, c2_ref, c3_ref, o_ref):
    outs, lses = [], []
    for g, c_ref in enumerate((c1_ref, c2_ref, c3_ref)):
        q = _bf16_round(q_ref[g, 0])
        kn = _bf16_round(kvn_ref[g, 0, 0])
        vn = _bf16_round(kvn_ref[g, 1, 0])
        kc = _bf16_round(c_ref[0, :, 0, 0])
        vc = _bf16_round(c_ref[0, :, 0, 1])
        s_c = jnp.sum(kc * q[None], axis=-1, keepdims=True)
        s_n = jnp.sum(kn * q, axis=-1, keepdims=True)
        m = jnp.maximum(jnp.max(s_c, axis=0), s_n)
        p_c = jnp.exp(s_c - m[None])
        p_n = jnp.exp(s_n - m)
        den = jnp.sum(p_c, axis=0) + p_n
        o = jnp.sum(_bf16_round(p_c / den[None]) * vc, axis=0) + _bf16_round(p_n / den) * vn
        outs.append(o)
        lses.append(m + jnp.log(den))
    mx = jnp.maximum(jnp.maximum(lses[0], lses[1]), lses[2])
    es = [jnp.exp(l - mx) for l in lses]
    tot = (es[0] + es[1]) + es[2]
    acc = outs[0] * (es[0] / tot)
    acc = acc + outs[1] * (es[1] / tot)
    acc = acc + outs[2] * (es[2] / tot)
    o_ref[0] = acc


def _attn_sample(q, kvn, caches):
    b = q.shape[1]
    nk = N_KEYS - 1
    views = [c.reshape(b, nk, DILATIONS[g], 2, N_HEADS, HEAD_DIM) for g, c in enumerate(caches)]
    cspec = pl.BlockSpec((1, nk, 1, 2, N_HEADS, HEAD_DIM), lambda i: (i, 0, 0, 0, 0, 0))
    return pl.pallas_call(
        _attn_sample_kernel,
        grid=(b,),
        in_specs=[
            pl.BlockSpec((N_GROUPS, 1, N_HEADS, HEAD_DIM), lambda i: (0, i, 0, 0)),
            pl.BlockSpec((N_GROUPS, 2, 1, N_HEADS, HEAD_DIM), lambda i: (0, 0, i, 0, 0)),
            cspec, cspec, cspec,
        ],
        out_specs=pl.BlockSpec((1, N_HEADS, HEAD_DIM), lambda i: (i, 0, 0)),
        out_shape=jax.ShapeDtypeStruct((b, N_HEADS, HEAD_DIM), F32),
        compiler_params=_params("parallel"),
        name="attn_sample",
    )(q, kvn, *views)


def _matmul_res_kernel(x_ref, a_ref, w_ref, o_ref):
    o_ref[...] = x_ref[...] + jnp.dot(a_ref[...].astype(BF16), w_ref[...], preferred_element_type=F32)


def _matmul_res(x, a, w):
    vm = pl.BlockSpec(memory_space=pltpu.VMEM)
    return pl.pallas_call(
        _matmul_res_kernel,
        in_specs=[vm] * 3,
        out_specs=vm,
        out_shape=jax.ShapeDtypeStruct(x.shape, F32),
        compiler_params=pltpu.CompilerParams(vmem_limit_bytes=VMEM_LIMIT_BYTES),
        name="matmul_res",
    )(x, a, w)


_SHIFT_BATCH_CHUNK = (32, 8, 2)


def _shift_copies(kvn_ref, c_refs, o_refs, sem):
    copies = []
    k = 0
    for g in range(N_GROUPS):
        c_ref, o_ref = c_refs[g], o_refs[g]
        b, w = c_ref.shape[0], c_ref.shape[1]
        step = _SHIFT_BATCH_CHUNK[g]
        for b0 in range(0, b, step):
            copies.append(pltpu.make_async_copy(
                c_ref.at[pl.ds(b0, step), pl.ds(1, w - 1)],
                o_ref.at[pl.ds(b0, step), pl.ds(0, w - 1)], sem.at[k]))
            k += 1
        for part in range(2):
            copies.append(pltpu.make_async_copy(
                kvn_ref.at[g, part], o_ref.at[:, w - 1, part], sem.at[k]))
            k += 1
    return copies


def _n_shift_copies(b):
    return sum(b // _SHIFT_BATCH_CHUNK[g] + 2 for g in range(N_GROUPS))


def _cache_shift_kernel(kvn_ref, c1_ref, c2_ref, c3_ref, o1_ref, o2_ref, o3_ref, sem):
    copies = _shift_copies(kvn_ref, (c1_ref, c2_ref, c3_ref), (o1_ref, o2_ref, o3_ref), sem)
    for cp in copies:
        cp.start()
    for cp in copies:
        cp.wait()


def _cache_shift(kvn, caches):
    b = kvn.shape[2]
    hbm = pl.BlockSpec(memory_space=pl.ANY)
    return pl.pallas_call(
        _cache_shift_kernel,
        in_specs=[hbm] * 4,
        out_specs=[hbm] * 3,
        out_shape=[jax.ShapeDtypeStruct(c.shape, c.dtype) for c in caches],
        scratch_shapes=[pltpu.SemaphoreType.DMA((_n_shift_copies(b),))],
        name="cache_shift",
    )(kvn, *caches)


def kernel(x_prompt, x_sample, state_h, state_conv, cache_kv_g1, cache_kv_g2, cache_kv_g3,
           ffn1_norm, ffn1_w_up, ffn1_w_down, mix_norm, ffn2_norm, ffn2_w_up, ffn2_w_down,
           rg_w_in, rg_conv_w, rg_conv_b, rg_gate_a_w, rg_gate_a_b, rg_gate_x_w, rg_gate_x_b,
           rg_lambda, rg_w_out, kv_norm, w_kv, attn_w_q, attn_w_o, final_norm):
    caches = [cache_kv_g1, cache_kv_g2, cache_kv_g3]
    bp, sp, d = x_prompt.shape
    db, ds, _ = x_sample.shape
    assert bp == 1 and ds == 1 and d == D_MODEL
    assert all(c.shape[1] == w for c, w in zip(caches, WINDOWS))
    past = PAST_LEN

    xp = x_prompt.reshape(sp, d)
    xs = x_sample.reshape(db, d)
    row = lambda v: v.reshape(1, -1)
    bf = lambda w: w.astype(BF16)

    half = ROPE_DIM // 2
    inv = ROPE_THETA ** (-jnp.arange(half, dtype=F32) / half)
    inv_lane = jnp.concatenate([inv, inv, jnp.zeros((LANES - ROPE_DIM,), F32)]).reshape(1, LANES)
    fg = row(final_norm)

    def ffn_pair(xp, xs, g, w_up, w_down, final=False):
        wu, wd = bf(w_up), bf(w_down)
        xp = _ffn(xp, row(g), wu, wd, fg, final=final, tm=512, tf=512)
        xs = _ffn(xs, row(g), wu, wd, fg, final=final, tm=db, tf=512)
        return xp, xs

    xp, xs = ffn_pair(xp, xs, ffn1_norm[0], ffn1_w_up[0], ffn1_w_down[0])
    rg = (row(mix_norm[0]), bf(rg_w_in[0]), rg_conv_w[0], row(rg_conv_b[0]), bf(rg_gate_a_w[0]),
          row(rg_gate_a_b[0]), bf(rg_gate_x_w[0]), row(rg_gate_x_b[0]), row(rg_lambda[0]),
          bf(rg_w_out[0]))
    xp, p_h, p_conv = _rg_prompt(xp, *rg, tm=256)
    xs, s_h, s_conv = _rg_sample(xs, *rg, state_h[0], state_conv[0].reshape(db, -1))
    xp, xs = ffn_pair(xp, xs, ffn2_norm[0], ffn2_w_up[0], ffn2_w_down[0])

    wkv = bf(w_kv)
    kvg = row(kv_norm)
    kv_p, p_kv, kv_s = [], [], []
    for g in range(N_GROUPS):
        common = dict(rope=(True, False), scale=(1.0, 1.0))
        kv, tail = _proj(xp, kvg, inv_lane, wkv, (g, N_GROUPS + g), dil=DILATIONS[g], pos0=0,
                         pos_stride=1, tm=256, out_dtype=BF16, tail=WINDOWS[g], **common)
        kv_p.append(kv)
        p_kv.append(jnp.swapaxes(tail, 0, 1).reshape(1, WINDOWS[g], 2, N_HEADS, HEAD_DIM))
        kv_s.append(_proj(xs, kvg, inv_lane, wkv, (g, N_GROUPS + g), dil=1, pos0=past,
                          pos_stride=0, tm=db, out_dtype=F32, **common))
    kvn = jnp.stack(kv_s).reshape(N_GROUPS, 2, db, N_HEADS, HEAD_DIM)
    s_kv = _cache_shift(kvn, caches)

    xp, xs = ffn_pair(xp, xs, ffn1_norm[1], ffn1_w_up[1], ffn1_w_down[1])
    wq = bf(attn_w_q[0])
    wo = bf(attn_w_o[0])
    qg = row(mix_norm[1])
    qcommon = dict(rope=(True,), scale=(HEAD_DIM ** -0.5,))
    og, lg, q_s = [], [], []
    for g in range(N_GROUPS):
        q = _proj(xp, qg, inv_lane, wq, (g,), dil=DILATIONS[g], pos0=0, pos_stride=1, tm=256,
                  out_dtype=BF16, **qcommon)
        o, lse = _attn_group(q, kv_p[g], g, tq=256)
        og.append(o)
        lg.append(lse)
        q_s.append(_proj(xs, qg, inv_lane, wq, (g,), dil=1, pos0=past, pos_stride=0, tm=db,
                         out_dtype=F32, **qcommon))
    xp = _merge_wo(xp, og, lg, wo, tm=256)
    qs = jnp.stack(q_s).reshape(N_GROUPS, db, N_HEADS, HEAD_DIM)
    xs = _matmul_res(xs, _attn_sample(qs, kvn, caches).reshape(db, GW), wo)
    xp, xs = ffn_pair(xp, xs, ffn2_norm[1], ffn2_w_up[1], ffn2_w_down[1], final=True)

    return (xp.reshape(bp, sp, d), xs.reshape(db, ds, d),
            p_h.reshape(1, bp, D_RNN), p_conv.reshape(1, bp, CONV_WIDTH - 1, D_RNN),
            p_kv[0], p_kv[1], p_kv[2],
            s_h.reshape(1, db, D_RNN), s_conv.reshape(1, db, CONV_WIDTH - 1, D_RNN),
            s_kv[0], s_kv[1], s_kv[2])
```

```python
import functools

import jax
import jax.numpy as jnp
from jax import lax
from jax.experimental import pallas as pl
from jax.experimental.pallas import tpu as pltpu

F32 = jnp.float32
BF16 = jnp.bfloat16

D_MODEL = 2048
D_FF = 5632
D_RNN = 2048
N_RG_BLOCKS = 16
RG_BLOCK = D_RNN // N_RG_BLOCKS
CONV_WIDTH = 4
RG_C = 8.0
HEAD_DIM = 128
N_HEADS = 16
N_GROUPS = 3
WINDOWS = (128, 512, 2048)
DILATIONS = (1, 4, 16)
N_KEYS = 129
ROPE_DIM = 32
ROPE_THETA = 500000.0
EPS = 1e-6
PAST_LEN = 8192
GW =N_HEADS * HEAD_DIM

LANES = 128
SUBLANES = 8
VMEM_LIMIT_BYTES = 56 * 1024 * 1024


def _params(*sem):
    return pltpu.CompilerParams(dimension_semantics=sem, vmem_limit_bytes=VMEM_LIMIT_BYTES)


def _rms(x, g):
    var = jnp.mean(x * x, axis=-1, keepdims=True)
    return (x * lax.rsqrt(var + EPS)) * g


def _softplus(z):
    return jnp.maximum(z, 0.0) + jnp.log1p(jnp.exp(-jnp.abs(z)))


def _ffn_kernel(x_ref, g_ref, wg_ref, wu_ref, wd_ref, fg_ref, o_ref, xn_ref, *, nf, final):
    j = pl.program_id(1)

    @pl.when(j == 0)
    def _():
        xn_ref[...] = _rms(x_ref[...], g_ref[...]).astype(BF16)
        o_ref[...] = jnp.zeros_like(o_ref)

    xn = xn_ref[...]
    gate = jnp.dot(xn, wg_ref[...], preferred_element_type=F32)
    up = jnp.dot(xn, wu_ref[...], preferred_element_type=F32)
    h = (gate * jax.nn.sigmoid(gate)) * up
    o_ref[...] += jnp.dot(h.astype(BF16), wd_ref[...], preferred_element_type=F32)

    @pl.when(j == nf - 1)
    def _():
        y = x_ref[...] + 0.5 * o_ref[...]
        if final:
            y = _rms(y, fg_ref[...])
        o_ref[...] = y


def _ffn(x, g, w_up, w_down, fg, *, final, tm, tf):
    m, d = x.shape
    f = w_down.shape[0]
    nf = f // tf
    return pl.pallas_call(
        functools.partial(_ffn_kernel, nf=nf, final=final),
        grid=(m // tm, nf),
        in_specs=[
            pl.BlockSpec((tm, d), lambda i, j: (i, 0)),
            pl.BlockSpec((1, d), lambda i, j: (0, 0)),
            pl.BlockSpec((d, tf), lambda i, j: (0, j)),
            pl.BlockSpec((d, tf), lambda i, j: (0, j + nf)),
            pl.BlockSpec((tf, d), lambda i, j: (j, 0)),
            pl.BlockSpec((1, d), lambda i, j: (0, 0)),
        ],
        out_specs=pl.BlockSpec((tm, d), lambda i, j: (i, 0)),
        out_shape=jax.ShapeDtypeStruct((m, d), F32),
        scratch_shapes=[pltpu.VMEM((tm, d), BF16)],
        compiler_params=_params("parallel", "arbitrary"),
        name="ffn",
    )(x, g, w_up, w_up, w_down, fg)


def _rg_gates(xc, gaw, gab, gxw, gxb, sp):
    xcb = xc.astype(BF16)
    r = jax.nn.sigmoid(jnp.dot(xcb, gaw, preferred_element_type=F32) + gab)
    ig = jax.nn.sigmoid(jnp.dot(xcb, gxw, preferred_element_type=F32) + gxb)
    log_a = (-RG_C * r) * sp
    a = jnp.exp(log_a)
    b = jnp.sqrt(-jnp.tanh(log_a) * (1.0 + a * a)) * (ig * xc)
    return a, b


def _rg_prompt_kernel(x_ref, g_ref, win_ref, cw_ref, cb_ref, gaw_ref, gab_ref, gxw_ref, gxb_ref,
                      lam_ref, wout_ref, o_ref, hlast_ref, conv_ref,
                      ext_ref, gate_ref, y_ref, h_ref, *, tm):
    i = pl.program_id(0)
    pad = SUBLANES

    @pl.when(i == 0)
    def _():
        ext_ref[0:pad, :] = jnp.zeros((pad, D_RNN), F32)
        h_ref[...] = jnp.zeros_like(h_ref)

    x = x_ref[...]
    xn = _rms(x, g_ref[...]).astype(BF16)
    gate_ref[...] = jnp.dot(xn, win_ref[:, 0:D_RNN], preferred_element_type=F32)
    ext_ref[pad:pad + tm, :] = jnp.dot(xn, win_ref[:, D_RNN:2 * D_RNN], preferred_element_type=F32)

    sub = lax.broadcasted_iota(jnp.int32, (tm // SUBLANES, SUBLANES, RG_BLOCK), 1)
    for n in range(N_RG_BLOCKS):
        cs = slice(n * RG_BLOCK, (n + 1) * RG_BLOCK)
        xc = cb_ref[:, cs] + ext_ref[pad - 3:pad - 3 + tm, cs] * cw_ref[0:1, cs]
        for j in range(1, CONV_WIDTH):
            xc = xc + ext_ref[pad - 3 + j:pad - 3 + j + tm, cs] * cw_ref[j:j + 1, cs]
        sp = _softplus(-lam_ref[:, cs])
        a, b = _rg_gates(xc, gaw_ref[n], gab_ref[:, cs], gxw_ref[n], gxb_ref[:, cs], sp)
        a = a.reshape(tm // SUBLANES, SUBLANES, RG_BLOCK)
        b = b.reshape(tm // SUBLANES, SUBLANES, RG_BLOCK)
        s = 1
        while s < SUBLANES:
            keep = sub >= s
            b = b + a * jnp.where(keep, pltpu.roll(b, s, 1), 0.0)
            a = a * jnp.where(keep, pltpu.roll(a, s, 1), 1.0)
            s *= 2
        carry = jnp.broadcast_to(h_ref[:, cs], (SUBLANES, RG_BLOCK))
        hs = []
        for k in range(tm // SUBLANES):
            hk = a[k] * carry + b[k]
            hs.append(hk)
            carry = jnp.broadcast_to(hk[SUBLANES - 1:SUBLANES, :], (SUBLANES, RG_BLOCK))
        h = jnp.concatenate(hs, axis=0)
        h_ref[:, cs] = carry[0:1, :]
        y_ref[:, cs] = (h * jax.nn.gelu(gate_ref[:, cs])).astype(BF16)

    o_ref[...] = x + jnp.dot(y_ref[...], wout_ref[...], preferred_element_type=F32)
    hlast_ref[...] = h_ref[...]
    conv_ref[...] = ext_ref[pad + tm - 3:pad + tm, :]
    ext_ref[0:pad, :] = ext_ref[tm:tm + pad, :]


def _rg_prompt(x, g, w_in, cw, cb, gaw, gab, gxw, gxb, lam, w_out, *, tm):
    m, d = x.shape
    const2 = lambda i: (0, 0)
    const3 = lambda i: (0, 0, 0)
    one = pl.Buffered(1)
    return pl.pallas_call(
        functools.partial(_rg_prompt_kernel, tm=tm),
        grid=(m // tm,),
        in_specs=[
            pl.BlockSpec((tm, d), lambda i: (i, 0)),
            pl.BlockSpec((1, d), const2),
            pl.BlockSpec((d, 2 * D_RNN), const2, pipeline_mode=one),
            pl.BlockSpec((CONV_WIDTH, D_RNN), const2),
            pl.BlockSpec((1, D_RNN), const2),
            pl.BlockSpec((N_RG_BLOCKS, RG_BLOCK, RG_BLOCK), const3),
            pl.BlockSpec((1, D_RNN), const2),
            pl.BlockSpec((N_RG_BLOCKS, RG_BLOCK, RG_BLOCK), const3),
            pl.BlockSpec((1, D_RNN), const2),
            pl.BlockSpec((1, D_RNN), const2),
            pl.BlockSpec((D_RNN, d), const2, pipeline_mode=one),
        ],
        out_specs=[
            pl.BlockSpec((tm, d), lambda i: (i, 0)),
            pl.BlockSpec((1, D_RNN), const2),
            pl.BlockSpec((CONV_WIDTH - 1, D_RNN), const2),
        ],
        out_shape=[
            jax.ShapeDtypeStruct((m, d), F32),
            jax.ShapeDtypeStruct((1, D_RNN), F32),
            jax.ShapeDtypeStruct((CONV_WIDTH - 1, D_RNN), F32),
        ],
        scratch_shapes=[
            pltpu.VMEM((tm + SUBLANES, D_RNN), F32),
            pltpu.VMEM((tm, D_RNN), F32),
            pltpu.VMEM((tm, D_RNN), BF16),
            pltpu.VMEM((1, D_RNN), F32),
        ],
        compiler_params=_params("arbitrary"),
        name="rg_prompt",
    )(x, g, w_in, cw, cb, gaw, gab, gxw, gxb, lam, w_out)


def _rg_sample_kernel(x_ref, g_ref, win_ref, cw_ref, cb_ref, gaw_ref, gab_ref, gxw_ref, gxb_ref,
                      lam_ref, wout_ref, h0_ref, c0_ref, o_ref, hnew_ref, cnew_ref, y_ref):
    x = x_ref[...]
    xn = _rms(x, g_ref[...]).astype(BF16)
    gate = jnp.dot(xn, win_ref[:, 0:D_RNN], preferred_element_type=F32)
    rec = jnp.dot(xn, win_ref[:, D_RNN:2 * D_RNN], preferred_element_type=F32)
    for n in range(N_RG_BLOCKS):
        cs = slice(n * RG_BLOCK, (n + 1) * RG_BLOCK)
        taps = [c0_ref[:, j * D_RNN + n * RG_BLOCK:j * D_RNN + (n + 1) * RG_BLOCK]
                for j in range(CONV_WIDTH - 1)] + [rec[:, cs]]
        xc = cb_ref[:, cs] + taps[0] * cw_ref[0:1, cs]
        for j in range(1, CONV_WIDTH):
            xc = xc + taps[j] * cw_ref[j:j + 1, cs]
        sp = _softplus(-lam_ref[:, cs])
        a, b = _rg_gates(xc, gaw_ref[n], gab_ref[:, cs], gxw_ref[n], gxb_ref[:, cs], sp)
        h = a * h0_ref[:, cs] + b
        hnew_ref[:, cs] = h
        y_ref[:, cs] = (h * jax.nn.gelu(gate[:, cs])).astype(BF16)
    o_ref[...] = x + jnp.dot(y_ref[...], wout_ref[...], preferred_element_type=F32)
    cnew_ref[:, 0:2 * D_RNN] = c0_ref[:, D_RNN:3 * D_RNN]
    cnew_ref[:, 2 * D_RNN:3 * D_RNN] = rec


def _rg_sample(x, g, w_in, cw, cb, gaw, gab, gxw, gxb, lam, w_out, h0, c0):
    m, d = x.shape
    vm = pl.BlockSpec(memory_space=pltpu.VMEM)
    return pl.pallas_call(
        _rg_sample_kernel,
        in_specs=[vm] * 13,
        out_specs=[vm] * 3,
        out_shape=[
            jax.ShapeDtypeStruct((m, d), F32),
            jax.ShapeDtypeStruct((m, D_RNN), F32),
            jax.ShapeDtypeStruct((m, (CONV_WIDTH - 1) * D_RNN), F32),
        ],
        scratch_shapes=[pltpu.VMEM((m, D_RNN), BF16)],
        compiler_params=pltpu.CompilerParams(vmem_limit_bytes=VMEM_LIMIT_BYTES),
        name="rg_sample",
    )(x, g, w_in, cw, cb, gaw, gab, gxw, gxb, lam, w_out, h0, c0)


_PROJ_CHUNK = 512


def _proj_kernel(*refs, tm, dil, n_mats, rope, scale, pos0, pos_stride, out_dtype, has_tail):
    x_ref, g_ref, inv_ref = refs[:3]
    w_refs = refs[3:3 + n_mats]
    o_ref = refs[3 + n_mats]
    tail_ref = refs[4 + n_mats] if has_tail else None
    slab_ref = refs[-1]
    i = pl.program_id(0)
    half = ROPE_DIM // 2

    xn = _rms(x_ref[...], g_ref[...]).astype(BF16)
    if any(rope):
        r = lax.broadcasted_iota(jnp.int32, (tm, LANES), 0)
        pos = pos0 + (i * tm + r) * pos_stride
        ang = pos.astype(F32) * inv_ref[...]
        lane = lax.broadcasted_iota(jnp.int32, (tm, LANES), 1)
        sin = jnp.sin(ang)
        cos = jnp.cos(ang)
        s_lo = jnp.where(lane < half, -sin, 0.0)
        s_hi = jnp.where((lane >= half) & (lane < ROPE_DIM), sin, 0.0)

    for k in range(n_mats):
        for c0 in range(0, GW, _PROJ_CHUNK):
            acc = jnp.dot(xn, w_refs[k][:, c0:c0 + _PROJ_CHUNK], preferred_element_type=F32)
            for c in range(c0, c0 + _PROJ_CHUNK, LANES):
                a = acc[:, c - c0:c - c0 + LANES]
                if rope[k]:
                    a = (a * cos + pltpu.roll(a, half, 1) * s_hi
                         + pltpu.roll(a, LANES - half, 1) * s_lo)
                if scale[k] != 1.0:
                    a = a * scale[k]
                if has_tail:
                    tail_ref[k, :, c:c + LANES] = a
                if dil == 1:
                    o_ref[k, 0, :, c:c + LANES] = a.astype(out_dtype)
                else:
                    s = c // LANES % slab_ref.shape[0]
                    slab_ref[s] = a
                    for r_ in range(dil):
                        o_ref[k, r_, :, c:c + LANES] = (
                            slab_ref[s, pl.ds(r_, tm // dil, stride=dil), :].astype(out_dtype))


def _proj(x, g, inv, w, col_blocks, *, dil, rope, scale, pos0, pos_stride, tm, out_dtype, tail=0):
    m, d = x.shape
    n_mats = len(col_blocks)
    assert m % tm == 0 and tm % dil == 0
    lrows = m // dil
    tail_rows = max(tail, tm) if tail else 0
    i0 = (m - tail_rows) // tm if tail else 0
    const2 = lambda i: (0, 0)
    in_specs = [pl.BlockSpec((tm, d), lambda i: (i, 0)), pl.BlockSpec((1, d), const2),
                pl.BlockSpec((1, LANES), const2)]
    for cb in col_blocks:
        in_specs.append(pl.BlockSpec((d, GW), lambda i, cb=cb: (0, cb), pipeline_mode=pl.Buffered(1)))
    out_specs = [pl.BlockSpec((n_mats, dil, tm // dil, GW), lambda i: (0, 0, i, 0))]
    out_shape = [jax.ShapeDtypeStruct((n_mats, dil, lrows, GW), out_dtype)]
    if tail:
        out_specs.append(pl.BlockSpec((n_mats, tm, GW), lambda i: (0, jnp.maximum(i - i0, 0), 0)))
        out_shape.append(jax.ShapeDtypeStruct((n_mats, tail_rows, GW), F32))
    outs = pl.pallas_call(
        functools.partial(_proj_kernel, tm=tm, dil=dil, n_mats=n_mats, rope=tuple(rope),
                          scale=tuple(scale), pos0=pos0, pos_stride=pos_stride,
                          out_dtype=out_dtype, has_tail=bool(tail)),
        grid=(m // tm,),
        in_specs=in_specs,
        out_specs=out_specs,
        out_shape=out_shape,
        scratch_shapes=[pltpu.VMEM((2, tm, LANES), F32)],
        compiler_params=_params("arbitrary"),
        name="norm_proj",
    )(x, g, inv, *([w] * n_mats))
    if tail:
        return outs[0], outs[1][:, tail_rows - tail:]
    return outs[0]


def _attn_kernel(q_ref, kc_ref, kp_ref, vc_ref, vp_ref, o_ref, lse_ref, *, tq, nb):
    ub = pl.program_id(0) % nb
    blk = LANES
    nsub = tq // blk
    row = lax.broadcasted_iota(jnp.int32, (blk, blk), 0)
    col = lax.broadcasted_iota(jnp.int32, (blk, blk), 1)
    lane = lax.broadcasted_iota(jnp.int32, (blk, LANES), 1)
    cur_ok = col <= row
    prev_tri = col >= row
    neg = -jnp.inf
    dn = (((1,), (1,)), ((), ()))

    first_ok = jnp.logical_and(prev_tri, ub > 0)
    lse_ref[...] = jnp.zeros_like(lse_ref)

    def head(h, carry):
        hs = pl.ds(pl.multiple_of(h * HEAD_DIM, HEAD_DIM), HEAD_DIM)
        rows = [slice(c * blk, (c + 1) * blk) for c in range(nsub)]
        ks = [kp_ref[:, hs]] + [kc_ref[rs, hs] for rs in rows]
        vs = [vp_ref[:, hs]] + [vc_ref[rs, hs] for rs in rows]
        scores = []
        for c, rs in enumerate(rows):
            q = q_ref[rs, hs]
            prev_ok = prev_tri if c > 0 else first_ok
            s_p = jnp.where(prev_ok, lax.dot_general(q, ks[c], dn, preferred_element_type=F32), neg)
            s_c = jnp.where(cur_ok, lax.dot_general(q, ks[c + 1], dn, preferred_element_type=F32), neg)
            scores.append((s_p, s_c))
        probs = []
        for c, rs in enumerate(rows):
            s_p, s_c = scores[c]
            m = jnp.max(jnp.maximum(s_p, s_c), axis=1, keepdims=True)
            p_p = jnp.exp(s_p - m)
            p_c = jnp.exp(s_c - m)
            den = jnp.sum(p_p + p_c, axis=1, keepdims=True)
            lse_ref[rs, :] = jnp.where(lane == h, m + jnp.log(den), lse_ref[rs, :])
            probs.append((p_p.astype(BF16), p_c.astype(BF16), 1.0 / den))
        for c, rs in enumerate(rows):
            p_p, p_c, rden = probs[c]
            o = (jnp.dot(p_p, vs[c], preferred_element_type=F32)
                 + jnp.dot(p_c, vs[c + 1], preferred_element_type=F32))
            o_ref[rs, hs] = o * rden
        return carry

    lax.fori_loop(0, N_HEADS, head, 0)


def _attn_group(q, kv, g, *, tq):
    _, dil, lrows, _ = q.shape
    s = dil * lrows
    nb = lrows // tq
    sub = tq // LANES
    qv = q.reshape(s, GW)
    kvv = kv.reshape(2, s, GW)
    prev_idx = lambda i: jnp.maximum(i * sub - 1, 0)
    o, lse = pl.pallas_call(
        functools.partial(_attn_kernel, tq=tq, nb=nb),
        grid=(s // tq,),
        in_specs=[
            pl.BlockSpec((tq, GW), lambda i: (i, 0)),
            pl.BlockSpec((None, tq, GW), lambda i: (0, i, 0)),
            pl.BlockSpec((None, LANES, GW), lambda i: (0, prev_idx(i), 0)),
            pl.BlockSpec((None, tq, GW), lambda i: (1, i, 0)),
            pl.BlockSpec((None, LANES, GW), lambda i: (1, prev_idx(i), 0)),
        ],
        out_specs=[
            pl.BlockSpec((tq, GW), lambda i: (i, 0)),
            pl.BlockSpec((tq, LANES), lambda i: (i, 0)),
        ],
        out_shape=[
            jax.ShapeDtypeStruct((s, GW), F32),
            jax.ShapeDtypeStruct((s, LANES), F32),
        ],
        compiler_params=_params("parallel"),
        name="attn_group",
    )(qv, kvv, kvv, kvv, kvv)
    return o.reshape(dil, lrows, GW), lse.reshape(dil, lrows, LANES)


def _merge_wo_kernel(x_ref, o1_ref, o2_ref, o3_ref, l1_ref, l2_ref, l3_ref, w_ref, out_ref,
                     onat_ref, lnat_ref, a_ref, *, tm):
    o_refs = (o1_ref, o2_ref, o3_ref)
    l_refs = (l1_ref, l2_ref, l3_ref)
    for g, dil in enumerate(DILATIONS):
        rows = tm // dil
        for r in range(dil):
            dst = pl.ds(r, rows, stride=dil) if dil > 1 else slice(None)
            lnat_ref[g, dst, :] = l_refs[g][r]
            for h in range(N_HEADS):
                onat_ref[g * N_HEADS + h, dst, :] = o_refs[g][r, :, h * HEAD_DIM:(h + 1) * HEAD_DIM]
    l1, l2, l3 = lnat_ref[0], lnat_ref[1], lnat_ref[2]
    mx = jnp.maximum(jnp.maximum(l1, l2), l3)
    e1, e2, e3 = jnp.exp(l1 - mx), jnp.exp(l2 - mx), jnp.exp(l3 - mx)
    tot = (e1 + e2) + e3
    w1, w2, w3 = e1 / tot, e2 / tot, e3 / tot
    for h in range(N_HEADS):
        a = onat_ref[h] * w1[:, h:h + 1]
        a = a + onat_ref[N_HEADS + h] * w2[:, h:h + 1]
        a = a + onat_ref[2 * N_HEADS + h] * w3[:, h:h + 1]
        a_ref[:, h * HEAD_DIM:(h + 1) * HEAD_DIM] = a.astype(BF16)
    out_ref[...] = x_ref[...] + jnp.dot(a_ref[...], w_ref[...], preferred_element_type=F32)


def _merge_wo(x, os_, ls_, w_o, *, tm):
    m, d = x.shape
    rowblk = lambda i: (i, 0)
    deint = lambda i: (0, i, 0)
    in_specs = [pl.BlockSpec((tm, d), rowblk)]
    in_specs += [pl.BlockSpec((dil, tm // dil, GW), deint) for dil in DILATIONS]
    in_specs += [pl.BlockSpec((dil, tm // dil, LANES), deint) for dil in DILATIONS]
    in_specs += [pl.BlockSpec((GW, d), lambda i: (0, 0), pipeline_mode=pl.Buffered(1))]
    return pl.pallas_call(
        functools.partial(_merge_wo_kernel, tm=tm),
        grid=(m // tm,),
        in_specs=in_specs,
        out_specs=pl.BlockSpec((tm, d), rowblk),
        out_shape=jax.ShapeDtypeStruct((m, d), F32),
        scratch_shapes=[pltpu.VMEM((N_GROUPS * N_HEADS, tm, LANES), F32),
                        pltpu.VMEM((N_GROUPS, tm, LANES), F32),
                        pltpu.VMEM((tm, GW), BF16)],
        compiler_params=_params("parallel"),
        name="merge_wo",
    )(x, *os_, *ls_, w_o)


def _bf16_round(v):
    return v.astype(BF16).astype(F32)


def _attn_sample_kernel(q_ref, kvn_ref, c1_ref, c2_ref, c3_ref, o_ref):
    outs, lses = [], []
    for g, c_ref in enumerate((c1_ref, c2_ref, c3_ref)):
        q = _bf16_round(q_ref[g, 0])
        kn = _bf16_round(kvn_ref[g, 0, 0])
        vn = _bf16_round(kvn_ref[g, 1, 0])
        kc = _bf16_round(c_ref[0, :, 0, 0])
        vc = _bf16_round(c_ref[0, :, 0, 1])
        s_c = jnp.sum(kc * q[None], axis=-1, keepdims=True)
        s_n = jnp.sum(kn * q, axis=-1, keepdims=True)
        m = jnp.maximum(jnp.max(s_c, axis=0), s_n)
        p_c = jnp.exp(s_c - m[None])
        p_n = jnp.exp(s_n - m)
        den = jnp.sum(p_c, axis=0) + p_n
        o = jnp.sum(_bf16_round(p_c / den[None]) * vc, axis=0) + _bf16_round(p_n / den) * vn
        outs.append(o)
        lses.append(m + jnp.log(den))
    mx = jnp.maximum(jnp.maximum(lses[0], lses[1]), lses[2])
    es = [jnp.exp(l - mx) for l in lses]
    tot = (es[0] + es[1]) + es[2]
    acc = outs[0] * (es[0] / tot)
    acc = acc + outs[1] * (es[1] / tot)
    acc = acc + outs[2] * (es[2] / tot)
    o_ref[0] = acc


def _attn_sample(q, kvn, caches):
    b = q.shape[1]
    nk = N_KEYS - 1
    views = [c.reshape(b, nk, DILATIONS[g], 2, N_HEADS, HEAD_DIM) for g, c in enumerate(caches)]
    cspec = pl.BlockSpec((1, nk, 1, 2, N_HEADS, HEAD_DIM), lambda i: (i, 0, 0, 0, 0, 0))
    return pl.pallas_call(
        _attn_sample_kernel,
        grid=(b,),
        in_specs=[
            pl.BlockSpec((N_GROUPS, 1, N_HEADS, HEAD_DIM), lambda i: (0, i, 0, 0)),
            pl.BlockSpec((N_GROUPS, 2, 1, N_HEADS, HEAD_DIM), lambda i: (0, 0, i, 0, 0)),
            cspec, cspec, cspec,
        ],
        out_specs=pl.BlockSpec((1, N_HEADS, HEAD_DIM), lambda i: (i, 0, 0)),
        out_shape=jax.ShapeDtypeStruct((b, N_HEADS, HEAD_DIM), F32),
        compiler_params=_params("parallel"),
        name="attn_sample",
    )(q, kvn, *views)


def _matmul_res_kernel(x_ref, a_ref, w_ref, o_ref):
    o_ref[...] = x_ref[...] + jnp.dot(a_ref[...].astype(BF16), w_ref[...], preferred_element_type=F32)


def _matmul_res(x, a, w):
    vm = pl.BlockSpec(memory_space=pltpu.VMEM)
    return pl.pallas_call(
        _matmul_res_kernel,
        in_specs=[vm] * 3,
        out_specs=vm,
        out_shape=jax.ShapeDtypeStruct(x.shape, F32),
        compiler_params=pltpu.CompilerParams(vmem_limit_bytes=VMEM_LIMIT_BYTES),
        name="matmul_res",
    )(x, a, w)


def _cache_shift_kernel(c_ref, nxt_ref, new_ref, o_ref, *, t, nblk):
    j = pl.program_id(1)
    o_ref[0, 0:t - 1] = c_ref[0, 1:t]

    @pl.when(j < nblk - 1)
    def _():
        o_ref[0, t - 1] = nxt_ref[0, 0]

    @pl.when(j == nblk - 1)
    def _():
        o_ref[0, t - 1] = new_ref[:, 0]


def _cache_shift(kv_new, cache, *, t):
    b, w = cache.shape[0], cache.shape[1]
    nblk = w // t
    tail = cache.shape[2:]
    zeros = (0,) * len(tail)
    return pl.pallas_call(
        functools.partial(_cache_shift_kernel, t=t, nblk=nblk),
        grid=(b, nblk),
        in_specs=[
            pl.BlockSpec((1, t) + tail, lambda i, j: (i, j) + zeros),
            pl.BlockSpec((1, 1) + tail, lambda i, j: (i, jnp.minimum((j + 1) * t, w - 1)) + zeros),
            pl.BlockSpec((2, 1) + tail[1:], lambda i, j: (0, i, 0, 0)),
        ],
        out_specs=pl.BlockSpec((1, t) + tail, lambda i, j: (i, j) + zeros),
        out_shape=jax.ShapeDtypeStruct(cache.shape, cache.dtype),
        compiler_params=_params("parallel", "arbitrary"),
        name="cache_shift",
    )(cache, cache, kv_new)


def kernel(x_prompt, x_sample, state_h, state_conv, cache_kv_g1, cache_kv_g2, cache_kv_g3,
           ffn1_norm, ffn1_w_up, ffn1_w_down, mix_norm, ffn2_norm, ffn2_w_up, ffn2_w_down,
           rg_w_in, rg_conv_w, rg_conv_b, rg_gate_a_w, rg_gate_a_b, rg_gate_x_w, rg_gate_x_b,
           rg_lambda, rg_w_out, kv_norm, w_kv, attn_w_q, attn_w_o, final_norm):
    caches = [cache_kv_g1, cache_kv_g2, cache_kv_g3]
    bp, sp, d = x_prompt.shape
    db, ds, _ = x_sample.shape
    assert bp == 1 and ds == 1 and d == D_MODEL
    assert all(c.shape[1] == w for c, w in zip(caches, WINDOWS))
    past = PAST_LEN

    xp = x_prompt.reshape(sp, d)
    xs = x_sample.reshape(db, d)
    row = lambda v: v.reshape(1, -1)
    bf = lambda w: w.astype(BF16)

    half = ROPE_DIM // 2
    inv = ROPE_THETA ** (-jnp.arange(half, dtype=F32) / half)
    inv_lane = jnp.concatenate([inv, inv, jnp.zeros((LANES - ROPE_DIM,), F32)]).reshape(1, LANES)
    fg = row(final_norm)

    def ffn_pair(xp, xs, g, w_up, w_down, final=False):
        wu, wd = bf(w_up), bf(w_down)
        xp = _ffn(xp, row(g), wu, wd, fg, final=final, tm=512, tf=512)
        xs = _ffn(xs, row(g), wu, wd, fg, final=final, tm=db, tf=512)
        return xp, xs

    xp, xs = ffn_pair(xp, xs, ffn1_norm[0], ffn1_w_up[0], ffn1_w_down[0])
    rg = (row(mix_norm[0]), bf(rg_w_in[0]), rg_conv_w[0], row(rg_conv_b[0]), bf(rg_gate_a_w[0]),
          row(rg_gate_a_b[0]), bf(rg_gate_x_w[0]), row(rg_gate_x_b[0]), row(rg_lambda[0]),
          bf(rg_w_out[0]))
    xp, p_h, p_conv = _rg_prompt(xp, *rg, tm=256)
    xs, s_h, s_conv = _rg_sample(xs, *rg, state_h[0], state_conv[0].reshape(db, -1))
    xp, xs = ffn_pair(xp, xs, ffn2_norm[0], ffn2_w_up[0], ffn2_w_down[0])

    wkv = bf(w_kv)
    kvg = row(kv_norm)
    kv_p, p_kv, kv_s = [], [], []
    for g in range(N_GROUPS):
        common = dict(rope=(True, False), scale=(1.0, 1.0))
        kv, tail = _proj(xp, kvg, inv_lane, wkv, (g, N_GROUPS + g), dil=DILATIONS[g], pos0=0,
                         pos_stride=1, tm=256, out_dtype=BF16, tail=WINDOWS[g], **common)
        kv_p.append(kv)
        p_kv.append(jnp.swapaxes(tail, 0, 1).reshape(1, WINDOWS[g], 2, N_HEADS, HEAD_DIM))
        kv_s.append(_proj(xs, kvg, inv_lane, wkv, (g, N_GROUPS + g), dil=1, pos0=past,
                          pos_stride=0, tm=db, out_dtype=F32, **common))
    kvn = jnp.stack(kv_s).reshape(N_GROUPS, 2, db, N_HEADS, HEAD_DIM)
    s_kv = [_cache_shift(kvn[g], caches[g], t=min(WINDOWS[g], 256)) for g in range(N_GROUPS)]

    xp, xs = ffn_pair(xp, xs, ffn1_norm[1], ffn1_w_up[1], ffn1_w_down[1])
    wq = bf(attn_w_q[0])
    wo = bf(attn_w_o[0])
    qg = row(mix_norm[1])
    qcommon = dict(rope=(True,), scale=(HEAD_DIM ** -0.5,))
    og, lg, q_s = [], [], []
    for g in range(N_GROUPS):
        q = _proj(xp, qg, inv_lane, wq, (g,), dil=DILATIONS[g], pos0=0, pos_stride=1, tm=512,
                  out_dtype=BF16, **qcommon)
        o, lse = _attn_group(q, kv_p[g], g, tq=512)
        og.append(o)
        lg.append(lse)
        q_s.append(_proj(xs, qg, inv_lane, wq, (g,), dil=1, pos0=past, pos_stride=0, tm=db,
                         out_dtype=F32, **qcommon))
    xp = _merge_wo(xp, og, lg, wo, tm=256)
    qs = jnp.stack(q_s).reshape(N_GROUPS, db, N_HEADS, HEAD_DIM)
    xs = _matmul_res(xs, _attn_sample(qs, kvn, caches).reshape(db, GW), wo)
    xp, xs = ffn_pair(xp, xs, ffn2_norm[1], ffn2_w_up[1], ffn2_w_down[1], final=True)

    return (xp.reshape(bp, sp, d), xs.reshape(db, ds, d),
            p_h.reshape(1, bp, D_RNN), p_conv.reshape(1, bp, CONV_WIDTH - 1, D_RNN),
            p_kv[0], p_kv[1], p_kv[2],
            s_h.reshape(1, db, D_RNN), s_conv.reshape(1, db, CONV_WIDTH - 1, D_RNN),
            s_kv[0], s_kv[1], s_kv[2])
```

```python
import functools
from typing import NamedTuple, Optional

import jax
import jax.numpy as jnp
from jax import lax
from jax.experimental import pallas as pl
from jax.experimental.pallas import tpu as pltpu

F32 = jnp.float32
BF16 = jnp.bfloat16

D_MODEL = 2048
D_FF = 5632
D_RNN = 2048
N_RG_BLOCKS = 16
RG_BLOCK = D_RNN // N_RG_BLOCKS
CONV_WIDTH = 4
RG_C = 8.0
HEAD_DIM = 128
N_HEADS = 16
N_GROUPS = 3
WINDOWS = (128, 512, 2048)
DILATIONS = (1, 4, 16)
N_KEYS = 129
ROPE_DIM = 32
ROPE_THETA = 500000.0
EPS = 1e-6
PAST_LEN = 8192
GW =N_HEADS * HEAD_DIM

LANES = 128
SUBLANES = 8
VMEM_LIMIT_BYTES = 60 * 1024 * 1024


def _params(*sem):
    return pltpu.CompilerParams(dimension_semantics=sem, vmem_limit_bytes=VMEM_LIMIT_BYTES)


def _rms(x, g):
    var = jnp.mean(x * x, axis=-1, keepdims=True)
    return (x * lax.rsqrt(var + EPS)) * g


def _softplus(z):
    return jnp.maximum(z, 0.0) + jnp.log1p(jnp.exp(-jnp.abs(z)))


class _ShiftJob(NamedTuple):
    cache: jax.Array
    prev: Optional[jax.Array]
    first: int
    count: int
    t: int


def _shift_rows(c_ref, nxt_ref, o_ref):
    t = c_ref.shape[1]
    o_ref[0, 0:t - 1] = c_ref[0, 1:t]
    o_ref[0, t - 1] = nxt_ref[0, 0]


def _with_shift(kernel_fn, n_in, n_out, n_shift_in):
    def wrapped(*refs):
        c_ref, nxt_ref = refs[n_in], refs[n_in + 1]
        outs_at = n_in + n_shift_in
        kernel_fn(*refs[:n_in], *refs[outs_at:outs_at + n_out], *refs[outs_at + n_out + 1:])
        _shift_rows(c_ref, nxt_ref, refs[outs_at + n_out])
    return wrapped


def _call(kernel_fn, *, name, grid, in_specs, args, out_specs, out_shape, scratch, sem, shift=None):
    aliases = {}
    if shift is not None:
        b, w = shift.cache.shape[:2]
        nblk = w // shift.t
        tail = shift.cache.shape[2:]
        zeros = (0,) * len(tail)

        def blk(i):
            k = shift.first + jnp.minimum(i, shift.count - 1)
            return k // nblk, k % nblk

        def nxt(i):
            bi, ji = blk(i)
            return bi, jnp.minimum((ji + 1) * shift.t, w - 1)

        shift_in = [pl.BlockSpec((1, shift.t) + tail, lambda i: blk(i) + zeros),
                    pl.BlockSpec((1, 1) + tail, lambda i: nxt(i) + zeros)]
        shift_args = [shift.cache, shift.cache]
        if shift.prev is not None:
            shift_in.append(pl.BlockSpec(memory_space=pl.ANY))
            shift_args.append(shift.prev)
            aliases = {len(in_specs) + 2: len(out_specs)}
        kernel_fn = _with_shift(kernel_fn, len(in_specs), len(out_specs), len(shift_in))
        in_specs = list(in_specs) + shift_in
        args = list(args) + shift_args
        out_specs = list(out_specs) + [pl.BlockSpec((1, shift.t) + tail, lambda i: blk(i) + zeros)]
        out_shape = list(out_shape) + [jax.ShapeDtypeStruct(shift.cache.shape, shift.cache.dtype)]
    return pl.pallas_call(
        kernel_fn, grid=grid, in_specs=in_specs, out_specs=out_specs, out_shape=out_shape,
        input_output_aliases=aliases, scratch_shapes=scratch, compiler_params=_params(sem),
        name=name,
    )(*args)


def _shift_only_kernel():
    pass


def _shift_blocks(job):
    return _call(_shift_only_kernel, name="cache_shift", grid=(job.count,), in_specs=[], args=[],
                 out_specs=[], out_shape=[], scratch=[], sem="arbitrary", shift=job)[-1]


def _insert_rows_kernel(*refs):
    n = len(refs) // 3
    for new_ref, o_ref in zip(refs[:n], refs[2 * n:]):
        o_ref[0, 0] = new_ref[:, 0]


def _insert_rows(kv_new, shifted):
    n = len(shifted)
    b = shifted[0].shape[0]
    tail = shifted[0].shape[2:]
    row_spec = lambda w: pl.BlockSpec((1, 1) + tail, lambda i: (i, w - 1, 0, 0, 0))
    return pl.pallas_call(
        _insert_rows_kernel,
        grid=(b,),
        in_specs=[pl.BlockSpec((2, 1) + tail[1:], lambda i: (0, i, 0, 0))] * n
        + [pl.BlockSpec(memory_space=pl.ANY)] * n,
        out_specs=[row_spec(s.shape[1]) for s in shifted],
        out_shape=[jax.ShapeDtypeStruct(s.shape, s.dtype) for s in shifted],
        input_output_aliases={n + k: k for k in range(n)},
        compiler_params=_params("arbitrary"),
        name="insert_rows",
    )(*kv_new, *shifted)


def _ffn_kernel(x_ref, g_ref, wg_ref, wu_ref, wd_ref, fg_ref, o_ref, *rest, nf, final, emit):
    xn_ref = rest[-1]
    j = pl.program_id(1)

    @pl.when(j == 0)
    def _():
        xn_ref[...] = _rms(x_ref[...], g_ref[...]).astype(BF16)
        o_ref[...] = jnp.zeros_like(o_ref)

    wg, wu, wd = wg_ref[...], wu_ref[...], wd_ref[...]
    if emit:
        wg, wu, wd = wg.astype(BF16), wu.astype(BF16), wd.astype(BF16)
        rest[0][...], rest[1][...], rest[2][...] = wg, wu, wd

    xn = xn_ref[...]
    gate = jnp.dot(xn, wg, preferred_element_type=F32)
    up = jnp.dot(xn, wu, preferred_element_type=F32)
    h = (gate * jax.nn.sigmoid(gate)) * up
    o_ref[...] += jnp.dot(h.astype(BF16), wd, preferred_element_type=F32)

    @pl.when(j == nf - 1)
    def _():
        y = x_ref[...] + 0.5 * o_ref[...]
        if final:
            y = _rms(y, fg_ref[...])
        o_ref[...] = y


def _ffn(x, g, weights, fg, *, layer, final, tm, tf):
    m, d = x.shape
    emit = len(weights) == 2
    nf = D_FF // tf
    if emit:
        w_up, w_down = weights
        w_args = [w_up, w_up, w_down]
        w_specs = [
            pl.BlockSpec((None, d, tf), lambda i, j: (layer, 0, j)),
            pl.BlockSpec((None, d, tf), lambda i, j: (layer, 0, j + nf)),
            pl.BlockSpec((None, tf, d), lambda i, j: (layer, j, 0)),
        ]
    else:
        w_args = list(weights)
        w_specs = [None] * 3
    cast_specs = [
        pl.BlockSpec((d, tf), lambda i, j: (0, j)),
        pl.BlockSpec((d, tf), lambda i, j: (0, j)),
        pl.BlockSpec((tf, d), lambda i, j: (j, 0)),
    ]
    cast_shapes = [jax.ShapeDtypeStruct((d, D_FF), BF16)] * 2 + [jax.ShapeDtypeStruct((D_FF, d), BF16)]
    outs = pl.pallas_call(
        functools.partial(_ffn_kernel, nf=nf, final=final, emit=emit),
        grid=(m // tm, nf),
        in_specs=[
            pl.BlockSpec((tm, d), lambda i, j: (i, 0)),
            pl.BlockSpec((1, d), lambda i, j: (0, 0)),
            *(w_specs if emit else cast_specs),
            pl.BlockSpec((1, d), lambda i, j: (0, 0)),
        ],
        out_specs=[pl.BlockSpec((tm, d), lambda i, j: (i, 0))] + (cast_specs if emit else []),
        out_shape=[jax.ShapeDtypeStruct((m, d), F32)] + (cast_shapes if emit else []),
        scratch_shapes=[pltpu.VMEM((tm, d), BF16)],
        compiler_params=_params("parallel", "arbitrary"),
        name="ffn",
    )(x, g, *w_args, fg)
    return (outs[0], tuple(outs[1:])) if emit else outs[0]


def _rg_gates(xc, gaw, gab, gxw, gxb, sp):
    xcb = xc.astype(BF16)
    r = jax.nn.sigmoid(jnp.dot(xcb, gaw, preferred_element_type=F32) + gab)
    ig = jax.nn.sigmoid(jnp.dot(xcb, gxw, preferred_element_type=F32) + gxb)
    log_a = (-RG_C * r) * sp
    a = jnp.exp(log_a)
    b = jnp.sqrt(-jnp.tanh(log_a) * (1.0 + a * a)) * (ig * xc)
    return a, b


def _rg_prompt_kernel(x_ref, g_ref, win_ref, cw_ref, cb_ref, gaw_ref, gab_ref, gxw_ref, gxb_ref,
                      lam_ref, wout_ref, o_ref, hlast_ref, conv_ref,
                      ext_ref, gate_ref, y_ref, h_ref, *, tm):
    i = pl.program_id(0)
    pad = SUBLANES

    @pl.when(i == 0)
    def _():
        ext_ref[0:pad, :] = jnp.zeros((pad, D_RNN), F32)
        h_ref[...] = jnp.zeros_like(h_ref)

    x = x_ref[...]
    xn = _rms(x, g_ref[...]).astype(BF16)
    gate_ref[...] = jnp.dot(xn, win_ref[:, 0:D_RNN], preferred_element_type=F32)
    ext_ref[pad:pad + tm, :] = jnp.dot(xn, win_ref[:, D_RNN:2 * D_RNN], preferred_element_type=F32)

    sub = lax.broadcasted_iota(jnp.int32, (tm // SUBLANES, SUBLANES, RG_BLOCK), 1)
    for n in range(N_RG_BLOCKS):
        cs = slice(n * RG_BLOCK, (n + 1) * RG_BLOCK)
        xc = cb_ref[:, cs] + ext_ref[pad - 3:pad - 3 + tm, cs] * cw_ref[0:1, cs]
        for j in range(1, CONV_WIDTH):
            xc = xc + ext_ref[pad - 3 + j:pad - 3 + j + tm, cs] * cw_ref[j:j + 1, cs]
        sp = _softplus(-lam_ref[:, cs])
        a, b = _rg_gates(xc, gaw_ref[n], gab_ref[:, cs], gxw_ref[n], gxb_ref[:, cs], sp)
        a = a.reshape(tm // SUBLANES, SUBLANES, RG_BLOCK)
        b = b.reshape(tm // SUBLANES, SUBLANES, RG_BLOCK)
        s = 1
        while s < SUBLANES:
            keep = sub >= s
            b = b + a * jnp.where(keep, pltpu.roll(b, s, 1), 0.0)
            a = a * jnp.where(keep, pltpu.roll(a, s, 1), 1.0)
            s *= 2
        carry = jnp.broadcast_to(h_ref[:, cs], (SUBLANES, RG_BLOCK))
        hs = []
        for k in range(tm // SUBLANES):
            hk = a[k] * carry + b[k]
            hs.append(hk)
            carry = jnp.broadcast_to(hk[SUBLANES - 1:SUBLANES, :], (SUBLANES, RG_BLOCK))
        h = jnp.concatenate(hs, axis=0)
        h_ref[:, cs] = carry[0:1, :]
        y_ref[:, cs] = (h * jax.nn.gelu(gate_ref[:, cs])).astype(BF16)

    o_ref[...] = x + jnp.dot(y_ref[...], wout_ref[...], preferred_element_type=F32)
    hlast_ref[...] = h_ref[...]
    conv_ref[...] = ext_ref[pad + tm - 3:pad + tm, :]
    ext_ref[0:pad, :] = ext_ref[tm:tm + pad, :]


def _rg_prompt(x, g, w_in, cw, cb, gaw, gab, gxw, gxb, lam, w_out, *, tm):
    m, d = x.shape
    const2 = lambda i: (0, 0)
    const3 = lambda i: (0, 0, 0)
    one = pl.Buffered(1)
    return pl.pallas_call(
        functools.partial(_rg_prompt_kernel, tm=tm),
        grid=(m // tm,),
        in_specs=[
            pl.BlockSpec((tm, d), lambda i: (i, 0)),
            pl.BlockSpec((1, d), const2),
            pl.BlockSpec((d, 2 * D_RNN), const2, pipeline_mode=one),
            pl.BlockSpec((CONV_WIDTH, D_RNN), const2),
            pl.BlockSpec((1, D_RNN), const2),
            pl.BlockSpec((N_RG_BLOCKS, RG_BLOCK, RG_BLOCK), const3),
            pl.BlockSpec((1, D_RNN), const2),
            pl.BlockSpec((N_RG_BLOCKS, RG_BLOCK, RG_BLOCK), const3),
            pl.BlockSpec((1, D_RNN), const2),
            pl.BlockSpec((1, D_RNN), const2),
            pl.BlockSpec((D_RNN, d), const2, pipeline_mode=one),
        ],
        out_specs=[
            pl.BlockSpec((tm, d), lambda i: (i, 0)),
            pl.BlockSpec((1, D_RNN), const2),
            pl.BlockSpec((CONV_WIDTH - 1, D_RNN), const2),
        ],
        out_shape=[
            jax.ShapeDtypeStruct((m, d), F32),
            jax.ShapeDtypeStruct((1, D_RNN), F32),
            jax.ShapeDtypeStruct((CONV_WIDTH - 1, D_RNN), F32),
        ],
        scratch_shapes=[
            pltpu.VMEM((tm + SUBLANES, D_RNN), F32),
            pltpu.VMEM((tm, D_RNN), F32),
            pltpu.VMEM((tm, D_RNN), BF16),
            pltpu.VMEM((1, D_RNN), F32),
        ],
        compiler_params=_params("arbitrary"),
        name="rg_prompt",
    )(x, g, w_in, cw, cb, gaw, gab, gxw, gxb, lam, w_out)


def _rg_sample_kernel(x_ref, g_ref, win_ref, cw_ref, cb_ref, gaw_ref, gab_ref, gxw_ref, gxb_ref,
                      lam_ref, wout_ref, h0_ref, c0_ref, o_ref, hnew_ref, cnew_ref, y_ref):
    x = x_ref[...]
    xn = _rms(x, g_ref[...]).astype(BF16)
    gate = jnp.dot(xn, win_ref[:, 0:D_RNN], preferred_element_type=F32)
    rec = jnp.dot(xn, win_ref[:, D_RNN:2 * D_RNN], preferred_element_type=F32)
    for n in range(N_RG_BLOCKS):
        cs = slice(n * RG_BLOCK, (n + 1) * RG_BLOCK)
        taps = [c0_ref[:, j * D_RNN + n * RG_BLOCK:j * D_RNN + (n + 1) * RG_BLOCK]
                for j in range(CONV_WIDTH - 1)] + [rec[:, cs]]
        xc = cb_ref[:, cs] + taps[0] * cw_ref[0:1, cs]
        for j in range(1, CONV_WIDTH):
            xc = xc + taps[j] * cw_ref[j:j + 1, cs]
        sp = _softplus(-lam_ref[:, cs])
        a, b = _rg_gates(xc, gaw_ref[n], gab_ref[:, cs], gxw_ref[n], gxb_ref[:, cs], sp)
        h = a * h0_ref[:, cs] + b
        hnew_ref[:, cs] = h
        y_ref[:, cs] = (h * jax.nn.gelu(gate[:, cs])).astype(BF16)
    o_ref[...] = x + jnp.dot(y_ref[...], wout_ref[...], preferred_element_type=F32)
    cnew_ref[:, 0:2 * D_RNN] = c0_ref[:, D_RNN:3 * D_RNN]
    cnew_ref[:, 2 * D_RNN:3 * D_RNN] = rec


def _rg_sample(x, g, w_in, cw, cb, gaw, gab, gxw, gxb, lam, w_out, h0, c0):
    m, d = x.shape
    vm = pl.BlockSpec(memory_space=pltpu.VMEM)
    return pl.pallas_call(
        _rg_sample_kernel,
        in_specs=[vm] * 13,
        out_specs=[vm] * 3,
        out_shape=[
            jax.ShapeDtypeStruct((m, d), F32),
            jax.ShapeDtypeStruct((m, D_RNN), F32),
            jax.ShapeDtypeStruct((m, (CONV_WIDTH - 1) * D_RNN), F32),
        ],
        scratch_shapes=[pltpu.VMEM((m, D_RNN), BF16)],
        compiler_params=pltpu.CompilerParams(vmem_limit_bytes=VMEM_LIMIT_BYTES),
        name="rg_sample",
    )(x, g, w_in, cw, cb, gaw, gab, gxw, gxb, lam, w_out, h0, c0)


_PROJ_CHUNK = 512


def _proj_kernel(*refs, tm, dil, n_mats, rope, scale, pos0, pos_stride, out_dtype, has_tail):
    x_ref, g_ref, inv_ref = refs[:3]
    w_refs = refs[3:3 + n_mats]
    o_ref = refs[3 + n_mats]
    tail_ref = refs[4 + n_mats] if has_tail else None
    slab_ref = refs[-1]
    i = pl.program_id(0)
    half = ROPE_DIM // 2

    xn = _rms(x_ref[...], g_ref[...]).astype(BF16)
    if any(rope):
        r = lax.broadcasted_iota(jnp.int32, (tm, LANES), 0)
        pos = pos0 + (i * tm + r) * pos_stride
        ang = pos.astype(F32) * inv_ref[...]
        lane = lax.broadcasted_iota(jnp.int32, (tm, LANES), 1)
        sin = jnp.sin(ang)
        cos = jnp.cos(ang)
        s_lo = jnp.where(lane < half, -sin, 0.0)
        s_hi = jnp.where((lane >= half) & (lane < ROPE_DIM), sin, 0.0)

    for k in range(n_mats):
        for c0 in range(0, GW, _PROJ_CHUNK):
            acc = jnp.dot(xn, w_refs[k][:, c0:c0 + _PROJ_CHUNK], preferred_element_type=F32)
            for c in range(c0, c0 + _PROJ_CHUNK, LANES):
                a = acc[:, c - c0:c - c0 + LANES]
                if rope[k]:
                    a = (a * cos + pltpu.roll(a, half, 1) * s_hi
                         + pltpu.roll(a, LANES - half, 1) * s_lo)
                if scale[k] != 1.0:
                    a = a * scale[k]
                if has_tail:
                    tail_ref[k, :, c:c + LANES] = a
                if dil == 1:
                    o_ref[k, 0, :, c:c + LANES] = a.astype(out_dtype)
                else:
                    s = c // LANES % slab_ref.shape[0]
                    slab_ref[s] = a
                    for r_ in range(dil):
                        o_ref[k, r_, :, c:c + LANES] = (
                            slab_ref[s, pl.ds(r_, tm // dil, stride=dil), :].astype(out_dtype))


def _proj(x, g, inv, w, col_blocks, *, dil, rope, scale, pos0, pos_stride, tm, out_dtype, tail=0,
          shift=None):
    m, d = x.shape
    n_mats = len(col_blocks)
    assert m % tm == 0 and tm % dil == 0
    lrows = m // dil
    tail_rows = max(tail, tm) if tail else 0
    i0 = (m - tail_rows) // tm if tail else 0
    const2 = lambda i: (0, 0)
    in_specs = [pl.BlockSpec((tm, d), lambda i: (i, 0)), pl.BlockSpec((1, d), const2),
                pl.BlockSpec((1, LANES), const2)]
    for cb in col_blocks:
        in_specs.append(pl.BlockSpec((d, GW), lambda i, cb=cb: (0, cb), pipeline_mode=pl.Buffered(1)))
    out_specs = [pl.BlockSpec((n_mats, dil, tm // dil, GW), lambda i: (0, 0, i, 0))]
    out_shape = [jax.ShapeDtypeStruct((n_mats, dil, lrows, GW), out_dtype)]
    if tail:
        out_specs.append(pl.BlockSpec((n_mats, tm, GW), lambda i: (0, jnp.maximum(i - i0, 0), 0)))
        out_shape.append(jax.ShapeDtypeStruct((n_mats, tail_rows, GW), F32))
    outs = _call(
        functools.partial(_proj_kernel, tm=tm, dil=dil, n_mats=n_mats, rope=tuple(rope),
                          scale=tuple(scale), pos0=pos0, pos_stride=pos_stride,
                          out_dtype=out_dtype, has_tail=bool(tail)),
        name="norm_proj", grid=(m // tm,), in_specs=in_specs, args=[x, g, inv] + [w] * n_mats,
        out_specs=out_specs, out_shape=out_shape, scratch=[pltpu.VMEM((2, tm, LANES), F32)],
        sem="arbitrary", shift=shift)
    return (outs[0], outs[1][:, tail_rows - tail:] if tail else None,
            outs[-1] if shift is not None else None)


def _attn_kernel(q_ref, kc_ref, kp_ref, vc_ref, vp_ref, o_ref, lse_ref, *, tq, nb):
    ub = pl.program_id(0) % nb
    blk = LANES
    nsub = tq // blk
    row = lax.broadcasted_iota(jnp.int32, (blk, blk), 0)
    col = lax.broadcasted_iota(jnp.int32, (blk, blk), 1)
    lane = lax.broadcasted_iota(jnp.int32, (blk, LANES), 1)
    cur_ok = col <= row
    prev_tri = col >= row
    neg = -jnp.inf
    dn = (((1,), (1,)), ((), ()))

    first_ok = jnp.logical_and(prev_tri, ub > 0)
    lse_ref[...] = jnp.zeros_like(lse_ref)

    def head(h, carry):
        hs = pl.ds(pl.multiple_of(h * HEAD_DIM, HEAD_DIM), HEAD_DIM)
        rows = [slice(c * blk, (c + 1) * blk) for c in range(nsub)]
        ks = [kp_ref[:, hs]] + [kc_ref[rs, hs] for rs in rows]
        vs = [vp_ref[:, hs]] + [vc_ref[rs, hs] for rs in rows]
        scores = []
        for c, rs in enumerate(rows):
            q = q_ref[rs, hs]
            prev_ok = prev_tri if c > 0 else first_ok
            s_p = jnp.where(prev_ok, lax.dot_general(q, ks[c], dn, preferred_element_type=F32), neg)
            s_c = jnp.where(cur_ok, lax.dot_general(q, ks[c + 1], dn, preferred_element_type=F32), neg)
            scores.append((s_p, s_c))
        probs = []
        for c, rs in enumerate(rows):
            s_p, s_c = scores[c]
            m = jnp.max(jnp.maximum(s_p, s_c), axis=1, keepdims=True)
            p_p = jnp.exp(s_p - m)
            p_c = jnp.exp(s_c - m)
            den = jnp.sum(p_p + p_c, axis=1, keepdims=True)
            lse_ref[rs, :] = jnp.where(lane == h, m + jnp.log(den), lse_ref[rs, :])
            probs.append((p_p.astype(BF16), p_c.astype(BF16), 1.0 / den))
        for c, rs in enumerate(rows):
            p_p, p_c, rden = probs[c]
            o = (jnp.dot(p_p, vs[c], preferred_element_type=F32)
                 + jnp.dot(p_c, vs[c + 1], preferred_element_type=F32))
            o_ref[rs, hs] = o * rden
        return carry

    lax.fori_loop(0, N_HEADS, head, 0)


def _attn_group(q, kv, *, tq, shift=None):
    _, dil, lrows, _ = q.shape
    s = dil * lrows
    nb = lrows // tq
    sub = tq // LANES
    qv = q.reshape(s, GW)
    kvv = kv.reshape(2, s, GW)
    prev_idx = lambda i: jnp.maximum(i * sub - 1, 0)
    outs = _call(
        functools.partial(_attn_kernel, tq=tq, nb=nb),
        name="attn_group", grid=(s // tq,),
        in_specs=[
            pl.BlockSpec((tq, GW), lambda i: (i, 0)),
            pl.BlockSpec((None, tq, GW), lambda i: (0, i, 0)),
            pl.BlockSpec((None, LANES, GW), lambda i: (0, prev_idx(i), 0)),
            pl.BlockSpec((None, tq, GW), lambda i: (1, i, 0)),
            pl.BlockSpec((None, LANES, GW), lambda i: (1, prev_idx(i), 0)),
        ],
        args=[qv, kvv, kvv, kvv, kvv],
        out_specs=[
            pl.BlockSpec((tq, GW), lambda i: (i, 0)),
            pl.BlockSpec((tq, LANES), lambda i: (i, 0)),
        ],
        out_shape=[
            jax.ShapeDtypeStruct((s, GW), F32),
            jax.ShapeDtypeStruct((s, LANES), F32),
        ],
        scratch=[], sem="arbitrary", shift=shift)
    return (outs[0].reshape(dil, lrows, GW), outs[1].reshape(dil, lrows, LANES),
            outs[-1] if shift is not None else None)


def _merge_wo_kernel(x_ref, o1_ref, o2_ref, o3_ref, l1_ref, l2_ref, l3_ref, w_ref, out_ref,
                     onat_ref, lnat_ref, a_ref, *, tm):
    o_refs = (o1_ref, o2_ref, o3_ref)
    l_refs = (l1_ref, l2_ref, l3_ref)
    for g, dil in enumerate(DILATIONS):
        rows = tm // dil
        for r in range(dil):
            dst = pl.ds(r, rows, stride=dil) if dil > 1 else slice(None)
            lnat_ref[g, dst, :] = l_refs[g][r]
            for h in range(N_HEADS):
                onat_ref[g * N_HEADS + h, dst, :] = o_refs[g][r, :, h * HEAD_DIM:(h + 1) * HEAD_DIM]
    l1, l2, l3 = lnat_ref[0], lnat_ref[1], lnat_ref[2]
    mx = jnp.maximum(jnp.maximum(l1, l2), l3)
    e1, e2, e3 = jnp.exp(l1 - mx), jnp.exp(l2 - mx), jnp.exp(l3 - mx)
    tot = (e1 + e2) + e3
    w1, w2, w3 = e1 / tot, e2 / tot, e3 / tot
    for h in range(N_HEADS):
        a = onat_ref[h] * w1[:, h:h + 1]
        a = a + onat_ref[N_HEADS + h] * w2[:, h:h + 1]
        a = a + onat_ref[2 * N_HEADS + h] * w3[:, h:h + 1]
        a_ref[:, h * HEAD_DIM:(h + 1) * HEAD_DIM] = a.astype(BF16)
    out_ref[...] = x_ref[...] + jnp.dot(a_ref[...], w_ref[...], preferred_element_type=F32)


def _merge_wo(x, os_, ls_, w_o, *, tm, shift=None):
    m, d = x.shape
    rowblk = lambda i: (i, 0)
    deint = lambda i: (0, i, 0)
    in_specs = [pl.BlockSpec((tm, d), rowblk)]
    in_specs += [pl.BlockSpec((dil, tm // dil, GW), deint) for dil in DILATIONS]
    in_specs += [pl.BlockSpec((dil, tm // dil, LANES), deint) for dil in DILATIONS]
    in_specs += [pl.BlockSpec((GW, d), lambda i: (0, 0), pipeline_mode=pl.Buffered(1))]
    outs = _call(
        functools.partial(_merge_wo_kernel, tm=tm),
        name="merge_wo", grid=(m // tm,), in_specs=in_specs, args=[x, *os_, *ls_, w_o],
        out_specs=[pl.BlockSpec((tm, d), rowblk)],
        out_shape=[jax.ShapeDtypeStruct((m, d), F32)],
        scratch=[pltpu.VMEM((N_GROUPS * N_HEADS, tm, LANES), F32),
                 pltpu.VMEM((N_GROUPS, tm, LANES), F32),
                 pltpu.VMEM((tm, GW), BF16)],
        sem="arbitrary", shift=shift)
    return outs[0], (outs[-1] if shift is not None else None)


def _bf16_round(v):
    return v.astype(BF16).astype(F32)


def _attn_sample_kernel(q_ref, kvn_ref, c1_ref, c2_ref, c3_ref, o_ref):
    outs, lses = [], []
    for g, c_ref in enumerate((c1_ref, c2_ref, c3_ref)):
        q = _bf16_round(q_ref[g, 0])
        kn = _bf16_round(kvn_ref[g, 0, 0])
        vn = _bf16_round(kvn_ref[g, 1, 0])
        kc = _bf16_round(c_ref[0, :, 0, 0])
        vc = _bf16_round(c_ref[0, :, 0, 1])
        s_c = jnp.sum(kc * q[None], axis=-1, keepdims=True)
        s_n = jnp.sum(kn * q, axis=-1, keepdims=True)
        m = jnp.maximum(jnp.max(s_c, axis=0), s_n)
        p_c = jnp.exp(s_c - m[None])
        p_n = jnp.exp(s_n - m)
        den = jnp.sum(p_c, axis=0) + p_n
        o = jnp.sum(_bf16_round(p_c / den[None]) * vc, axis=0) + _bf16_round(p_n / den) * vn
        outs.append(o)
        lses.append(m + jnp.log(den))
    mx = jnp.maximum(jnp.maximum(lses[0], lses[1]), lses[2])
    es = [jnp.exp(l - mx) for l in lses]
    tot = (es[0] + es[1]) + es[2]
    acc = outs[0] * (es[0] / tot)
    acc = acc + outs[1] * (es[1] / tot)
    acc = acc + outs[2] * (es[2] / tot)
    o_ref[0] = acc


def _attn_sample(q, kvn, caches):
    b = q.shape[1]
    nk = N_KEYS - 1
    views = [c.reshape(b, nk, DILATIONS[g], 2, N_HEADS, HEAD_DIM) for g, c in enumerate(caches)]
    cspec = pl.BlockSpec((1, nk, 1, 2, N_HEADS, HEAD_DIM), lambda i: (i, 0, 0, 0, 0, 0))
    return pl.pallas_call(
        _attn_sample_kernel,
        grid=(b,),
        in_specs=[
            pl.BlockSpec((N_GROUPS, 1, N_HEADS, HEAD_DIM), lambda i: (0, i, 0, 0)),
            pl.BlockSpec((N_GROUPS, 2, 1, N_HEADS, HEAD_DIM), lambda i: (0, 0, i, 0, 0)),
            cspec, cspec, cspec,
        ],
        out_specs=pl.BlockSpec((1, N_HEADS, HEAD_DIM), lambda i: (i, 0, 0)),
        out_shape=jax.ShapeDtypeStruct((b, N_HEADS, HEAD_DIM), F32),
        compiler_params=_params("parallel"),
        name="attn_sample",
    )(q, kvn, *views)


def _matmul_res_kernel(x_ref, a_ref, w_ref, o_ref):
    o_ref[...] = x_ref[...] + jnp.dot(a_ref[...].astype(BF16), w_ref[...], preferred_element_type=F32)


def _matmul_res(x, a, w):
    vm = pl.BlockSpec(memory_space=pltpu.VMEM)
    return pl.pallas_call(
        _matmul_res_kernel,
        in_specs=[vm] * 3,
        out_specs=vm,
        out_shape=jax.ShapeDtypeStruct(x.shape, F32),
        compiler_params=pltpu.CompilerParams(vmem_limit_bytes=VMEM_LIMIT_BYTES),
        name="matmul_res",
    )(x, a, w)


def kernel(x_prompt, x_sample, state_h, state_conv, cache_kv_g1, cache_kv_g2, cache_kv_g3,
           ffn1_norm, ffn1_w_up, ffn1_w_down, mix_norm, ffn2_norm, ffn2_w_up, ffn2_w_down,
           rg_w_in, rg_conv_w, rg_conv_b, rg_gate_a_w, rg_gate_a_b, rg_gate_x_w, rg_gate_x_b,
           rg_lambda, rg_w_out, kv_norm, w_kv, attn_w_q, attn_w_o, final_norm):
    caches = [cache_kv_g1, cache_kv_g2, cache_kv_g3]
    bp, sp, d = x_prompt.shape
    db, ds, _ = x_sample.shape
    assert bp == 1 and ds == 1 and d == D_MODEL
    assert all(c.shape[1] == w for c, w in zip(caches, WINDOWS))
    past = PAST_LEN

    xp = x_prompt.reshape(sp, d)
    xs = x_sample.reshape(db, d)
    row = lambda v: v.reshape(1, -1)
    bf = lambda w: w.astype(BF16)

    half = ROPE_DIM // 2
    inv = ROPE_THETA ** (-jnp.arange(half, dtype=F32) / half)
    inv_lane = jnp.concatenate([inv, inv, jnp.zeros((LANES - ROPE_DIM,), F32)]).reshape(1, LANES)
    fg = row(final_norm)

    def ffn_pair(xp, xs, g, w_up, w_down, layer, final=False):
        xs, w_bf = _ffn(xs, row(g[layer]), (w_up, w_down), fg, layer=layer, final=final, tm=db, tf=512)
        xp = _ffn(xp, row(g[layer]), w_bf, fg, layer=layer, final=final, tm=512, tf=512)
        return xp, xs

    t3 = 256
    nblk3 = db * (WINDOWS[2] // t3)
    shifted3, next_blk = None, 0

    def ride(steps):
        nonlocal next_blk
        count = min(steps, nblk3 - next_blk)
        job = _ShiftJob(caches[2], shifted3, next_blk, count, t3)
        next_blk += count
        return job

    xp, xs = ffn_pair(xp, xs, ffn1_norm, ffn1_w_up, ffn1_w_down, 0)
    rg = (row(mix_norm[0]), bf(rg_w_in[0]), rg_conv_w[0], row(rg_conv_b[0]), bf(rg_gate_a_w[0]),
          row(rg_gate_a_b[0]), bf(rg_gate_x_w[0]), row(rg_gate_x_b[0]), row(rg_lambda[0]),
          bf(rg_w_out[0]))
    xp, p_h, p_conv = _rg_prompt(xp, *rg, tm=256)
    xs, s_h, s_conv = _rg_sample(xs, *rg, state_h[0], state_conv[0].reshape(db, -1))
    xp, xs = ffn_pair(xp, xs, ffn2_norm, ffn2_w_up, ffn2_w_down, 0)

    wkv = bf(w_kv)
    kvg = row(kv_norm)
    tm_p = 256
    kv_p, p_kv, kv_s = [], [], []
    for g in range(N_GROUPS):
        common = dict(rope=(True, False), scale=(1.0, 1.0))
        kv, tail, shifted3 = _proj(xp, kvg, inv_lane, wkv, (g, N_GROUPS + g), dil=DILATIONS[g],
                                   pos0=0, pos_stride=1, tm=tm_p, out_dtype=BF16, tail=WINDOWS[g],
                                   shift=ride(sp // tm_p), **common)
        kv_p.append(kv)
        p_kv.append(jnp.swapaxes(tail, 0, 1).reshape(1, WINDOWS[g], 2, N_HEADS, HEAD_DIM))
        kv_s.append(_proj(xs, kvg, inv_lane, wkv, (g, N_GROUPS + g), dil=1, pos0=past,
                          pos_stride=0, tm=db, out_dtype=F32, **common)[0])
    kvn = jnp.stack(kv_s).reshape(N_GROUPS, 2, db, N_HEADS, HEAD_DIM)

    xp, xs = ffn_pair(xp, xs, ffn1_norm, ffn1_w_up, ffn1_w_down, 1)
    wq = bf(attn_w_q[0])
    wo = bf(attn_w_o[0])
    qg = row(mix_norm[1])
    qcommon = dict(rope=(True,), scale=(HEAD_DIM ** -0.5,))
    tq = 512
    og, lg, q_s = [], [], []
    for g in range(N_GROUPS):
        q, _, shifted3 = _proj(xp, qg, inv_lane, wq, (g,), dil=DILATIONS[g], pos0=0, pos_stride=1,
                               tm=tm_p, out_dtype=BF16, shift=ride(sp // tm_p), **qcommon)
        o, lse, shifted3 = _attn_group(q, kv_p[g], tq=tq, shift=ride(sp // tq))
        og.append(o)
        lg.append(lse)
        q_s.append(_proj(xs, qg, inv_lane, wq, (g,), dil=1, pos0=past, pos_stride=0, tm=db,
                         out_dtype=F32, **qcommon)[0])
    xp, shifted3 = _merge_wo(xp, og, lg, wo, tm=tm_p, shift=ride(sp // tm_p))
    assert next_blk == nblk3
    qs = jnp.stack(q_s).reshape(N_GROUPS, db, N_HEADS, HEAD_DIM)
    xs = _matmul_res(xs, _attn_sample(qs, kvn, caches).reshape(db, GW), wo)
    xp, xs = ffn_pair(xp, xs, ffn2_norm, ffn2_w_up, ffn2_w_down, 1, final=True)

    shifted = [
        _shift_blocks(_ShiftJob(caches[g], None, 0, db * (WINDOWS[g] // t), t))
        for g, t in ((0, WINDOWS[0]), (1, 256))
    ] + [shifted3]
    s_kv = _insert_rows([kvn[g] for g in range(N_GROUPS)], shifted)

    return (xp.reshape(bp, sp, d), xs.reshape(db, ds, d),
            p_h.reshape(1, bp, D_RNN), p_conv.reshape(1, bp, CONV_WIDTH - 1, D_RNN),
            p_kv[0], p_kv[1], p_kv[2],
            s_h.reshape(1, db, D_RNN), s_conv.reshape(1, db, CONV_WIDTH - 1, D_RNN),
            s_kv[0], s_kv[1], s_kv[2])
```

```python
import functools
from typing import NamedTuple, Optional

import jax
import jax.numpy as jnp
from jax import lax
from jax.experimental import pallas as pl
from jax.experimental.pallas import tpu as pltpu

F32 = jnp.float32
BF16 = jnp.bfloat16

D_MODEL = 2048
D_FF = 5632
D_RNN = 2048
N_RG_BLOCKS = 16
RG_BLOCK = D_RNN // N_RG_BLOCKS
CONV_WIDTH = 4
RG_C = 8.0
HEAD_DIM = 128
N_HEADS = 16
N_GROUPS = 3
WINDOWS = (128, 512, 2048)
DILATIONS = (1, 4, 16)
N_KEYS = 129
ROPE_DIM = 32
ROPE_THETA = 500000.0
EPS = 1e-6
PAST_LEN = 8192
GW =N_HEADS * HEAD_DIM

LANES = 128
SUBLANES = 8
VMEM_LIMIT_BYTES = 60 * 1024 * 1024


def _params(*sem):
    return pltpu.CompilerParams(dimension_semantics=sem, vmem_limit_bytes=VMEM_LIMIT_BYTES)


def _rms(x, g):
    var = jnp.mean(x * x, axis=-1, keepdims=True)
    return (x * lax.rsqrt(var + EPS)) * g


def _softplus(z):
    return jnp.maximum(z, 0.0) + jnp.log1p(jnp.exp(-jnp.abs(z)))


def _sigmoid(z):
    return 0.5 * jnp.tanh(0.5 * z) + 0.5


class _ShiftJob(NamedTuple):
    cache: jax.Array
    prev: Optional[jax.Array]
    first: int
    count: int
    t: int


def _shift_rows(c_ref, nxt_ref, o_ref):
    t = c_ref.shape[1]
    o_ref[0, 0:t - 1] = c_ref[0, 1:t]
    o_ref[0, t - 1] = nxt_ref[0, 0]


def _with_shift(kernel_fn, n_in, n_out, n_shift_in, inline):
    def wrapped(*refs):
        outs_at = n_in + n_shift_in
        rider = functools.partial(_shift_rows, refs[n_in], refs[n_in + 1], refs[outs_at + n_out])
        core = (*refs[:n_in], *refs[outs_at:outs_at + n_out], *refs[outs_at + n_out + 1:])
        if inline:
            kernel_fn(*core, rider=rider)
        else:
            rider()
            kernel_fn(*core)
    return wrapped


def _call(kernel_fn, *, name, grid, in_specs, args, out_specs, out_shape, scratch, sem, shift=None,
          inline_rider=False):
    aliases = {}
    if shift is not None:
        b, w = shift.cache.shape[:2]
        nblk = w // shift.t
        tail = shift.cache.shape[2:]
        zeros = (0,) * len(tail)

        def blk(*idx):
            step = idx[0]
            for extent, i in zip(grid[1:], idx[1:]):
                step = step * extent + i
            k = shift.first + jnp.minimum(step, shift.count - 1)
            return k // nblk, k % nblk

        def nxt(*idx):
            bi, ji = blk(*idx)
            return bi, jnp.minimum((ji + 1) * shift.t, w - 1)

        shift_in = [pl.BlockSpec((1, shift.t) + tail, lambda *idx: blk(*idx) + zeros),
                    pl.BlockSpec((1, 1) + tail, lambda *idx: nxt(*idx) + zeros)]
        shift_args = [shift.cache, shift.cache]
        if shift.prev is not None:
            shift_in.append(pl.BlockSpec(memory_space=pl.ANY))
            shift_args.append(shift.prev)
            aliases = {len(in_specs) + 2: len(out_specs)}
        kernel_fn = _with_shift(kernel_fn, len(in_specs), len(out_specs), len(shift_in),
                                inline_rider)
        in_specs = list(in_specs) + shift_in
        args = list(args) + shift_args
        out_specs = list(out_specs) + [
            pl.BlockSpec((1, shift.t) + tail, lambda *idx: blk(*idx) + zeros)]
        out_shape = list(out_shape) + [jax.ShapeDtypeStruct(shift.cache.shape, shift.cache.dtype)]
    sems = (sem,) if isinstance(sem, str) else sem
    return pl.pallas_call(
        kernel_fn, grid=grid, in_specs=in_specs, out_specs=out_specs, out_shape=out_shape,
        input_output_aliases=aliases, scratch_shapes=scratch, compiler_params=_params(*sems),
        name=name,
    )(*args)


def _insert_rows_kernel(*refs):
    n = len(refs) // 3
    for new_ref, o_ref in zip(refs[:n], refs[2 * n:]):
        o_ref[0, 0] = new_ref[:, 0]


def _insert_rows(kv_new, shifted):
    n = len(shifted)
    b = shifted[0].shape[0]
    tail = shifted[0].shape[2:]
    row_spec = lambda w: pl.BlockSpec((1, 1) + tail, lambda i: (i, w - 1, 0, 0, 0))
    return pl.pallas_call(
        _insert_rows_kernel,
        grid=(b,),
        in_specs=[pl.BlockSpec((2, 1) + tail[1:], lambda i: (0, i, 0, 0))] * n
        + [pl.BlockSpec(memory_space=pl.ANY)] * n,
        out_specs=[row_spec(s.shape[1]) for s in shifted],
        out_shape=[jax.ShapeDtypeStruct(s.shape, s.dtype) for s in shifted],
        input_output_aliases={n + k: k for k in range(n)},
        compiler_params=_params("arbitrary"),
        name="insert_rows",
    )(*kv_new, *shifted)


def _ffn_kernel(x_ref, g_ref, wg_ref, wu_ref, wd_ref, fg_ref, o_ref, *rest, nf, final, emit,
                rider=None):
    xn_ref = rest[-1]
    j = pl.program_id(1)

    @pl.when(j == 0)
    def _():
        xn_ref[...] = _rms(x_ref[...], g_ref[...]).astype(BF16)
        o_ref[...] = jnp.zeros_like(o_ref)

    if rider is not None:
        rider()
    wg, wu, wd = wg_ref[...], wu_ref[...], wd_ref[...]
    if emit:
        wg, wu, wd = wg.astype(BF16), wu.astype(BF16), wd.astype(BF16)
        rest[0][...], rest[1][...], rest[2][...] = wg, wu, wd

    xn = xn_ref[...]
    gate = jnp.dot(xn, wg, preferred_element_type=F32)
    up = jnp.dot(xn, wu, preferred_element_type=F32)
    h = (gate * jax.nn.sigmoid(gate)) * up
    o_ref[...] += jnp.dot(h.astype(BF16), wd, preferred_element_type=F32)

    @pl.when(j == nf - 1)
    def _():
        y = x_ref[...] + 0.5 * o_ref[...]
        if final:
            y = _rms(y, fg_ref[...])
        o_ref[...] = y


def _ffn(x, g, weights, fg, *, layer, final, tm, tf, shift=None):
    m, d = x.shape
    emit = len(weights) == 2
    nf = D_FF // tf
    if emit:
        w_up, w_down = weights
        w_args = [w_up, w_up, w_down]
        w_specs = [
            pl.BlockSpec((None, d, tf), lambda i, j: (layer, 0, j)),
            pl.BlockSpec((None, d, tf), lambda i, j: (layer, 0, j + nf)),
            pl.BlockSpec((None, tf, d), lambda i, j: (layer, j, 0)),
        ]
    else:
        w_args = list(weights)
        w_specs = [None] * 3
    cast_specs = [
        pl.BlockSpec((d, tf), lambda i, j: (0, j)),
        pl.BlockSpec((d, tf), lambda i, j: (0, j)),
        pl.BlockSpec((tf, d), lambda i, j: (j, 0)),
    ]
    cast_shapes = [jax.ShapeDtypeStruct((d, D_FF), BF16)] * 2 + [jax.ShapeDtypeStruct((D_FF, d), BF16)]
    outs = _call(
        functools.partial(_ffn_kernel, nf=nf, final=final, emit=emit),
        name="ffn", grid=(m // tm, nf),
        in_specs=[
            pl.BlockSpec((tm, d), lambda i, j: (i, 0)),
            pl.BlockSpec((1, d), lambda i, j: (0, 0)),
            *(w_specs if emit else cast_specs),
            pl.BlockSpec((1, d), lambda i, j: (0, 0)),
        ],
        args=[x, g, *w_args, fg],
        out_specs=[pl.BlockSpec((tm, d), lambda i, j: (i, 0))] + (cast_specs if emit else []),
        out_shape=[jax.ShapeDtypeStruct((m, d), F32)] + (cast_shapes if emit else []),
        scratch=[pltpu.VMEM((tm, d), BF16)], sem=("arbitrary", "arbitrary"), shift=shift,
        inline_rider=True)
    return (outs[0], tuple(outs[1:4]) if emit else None, outs[-1] if shift is not None else None)


def _rg_gates(xc, gaw, gab, gxw, gxb, sp):
    xcb = xc.astype(BF16)
    r = _sigmoid(jnp.dot(xcb, gaw, preferred_element_type=F32) + gab)
    ig = _sigmoid(jnp.dot(xcb, gxw, preferred_element_type=F32) + gxb)
    log_a = (-RG_C * r) * sp
    a = jnp.exp(log_a)
    b = jnp.sqrt(-jnp.tanh(log_a) * (1.0 + a * a)) * (ig * xc)
    return a, b


def _rg_prompt_kernel(x_ref, g_ref, win_ref, cw_ref, cb_ref, gaw_ref, gab_ref, gxw_ref, gxb_ref,
                      lam_ref, wout_ref, o_ref, hlast_ref, conv_ref,
                      ext_ref, gate_ref, y_ref, h_ref, *, tm):
    i = pl.program_id(0)
    pad = SUBLANES

    @pl.when(i == 0)
    def _():
        ext_ref[0:pad, :] = jnp.zeros((pad, D_RNN), F32)
        h_ref[...] = jnp.zeros_like(h_ref)

    x = x_ref[...]
    y_ref[...] = _rms(x, g_ref[...]).astype(BF16)
    gate_ref[...] = jnp.dot(y_ref[...], win_ref[:, 0:D_RNN], preferred_element_type=F32)
    ext_ref[pad:pad + tm, :] = jnp.dot(y_ref[...], win_ref[:, D_RNN:2 * D_RNN],
                                       preferred_element_type=F32)

    sub = lax.broadcasted_iota(jnp.int32, (tm // SUBLANES, SUBLANES, RG_BLOCK), 1)
    for n in range(N_RG_BLOCKS):
        cs = slice(n * RG_BLOCK, (n + 1) * RG_BLOCK)
        xc = cb_ref[:, cs] + ext_ref[pad - 3:pad - 3 + tm, cs] * cw_ref[0:1, cs]
        for j in range(1, CONV_WIDTH):
            xc = xc + ext_ref[pad - 3 + j:pad - 3 + j + tm, cs] * cw_ref[j:j + 1, cs]
        sp = _softplus(-lam_ref[:, cs])
        a, b = _rg_gates(xc, gaw_ref[n], gab_ref[:, cs], gxw_ref[n], gxb_ref[:, cs], sp)
        a = a.reshape(tm // SUBLANES, SUBLANES, RG_BLOCK)
        b = b.reshape(tm // SUBLANES, SUBLANES, RG_BLOCK)
        s = 1
        while s < SUBLANES:
            keep = sub >= s
            b = b + a * jnp.where(keep, pltpu.roll(b, s, 1), 0.0)
            a = a * jnp.where(keep, pltpu.roll(a, s, 1), 1.0)
            s *= 2
        carry = jnp.broadcast_to(h_ref[:, cs], (SUBLANES, RG_BLOCK))
        hs = []
        for k in range(tm // SUBLANES):
            hk = a[k] * carry + b[k]
            hs.append(hk)
            carry = jnp.broadcast_to(hk[SUBLANES - 1:SUBLANES, :], (SUBLANES, RG_BLOCK))
        h = jnp.concatenate(hs, axis=0)
        h_ref[:, cs] = carry[0:1, :]
        y_ref[:, cs] = (h * jax.nn.gelu(gate_ref[:, cs])).astype(BF16)

    o_ref[...] = x + jnp.dot(y_ref[...], wout_ref[...], preferred_element_type=F32)
    hlast_ref[...] = h_ref[...]
    conv_ref[...] = ext_ref[pad + tm - 3:pad + tm, :]
    ext_ref[0:pad, :] = ext_ref[tm:tm + pad, :]


def _rg_prompt(x, g, w_in, cw, cb, gaw, gab, gxw, gxb, lam, w_out, *, tm):
    m, d = x.shape
    const2 = lambda i: (0, 0)
    const3 = lambda i: (0, 0, 0)
    one = pl.Buffered(1)
    return pl.pallas_call(
        functools.partial(_rg_prompt_kernel, tm=tm),
        grid=(m // tm,),
        in_specs=[
            pl.BlockSpec((tm, d), lambda i: (i, 0)),
            pl.BlockSpec((1, d), const2),
            pl.BlockSpec((d, 2 * D_RNN), const2, pipeline_mode=one),
            pl.BlockSpec((CONV_WIDTH, D_RNN), const2),
            pl.BlockSpec((1, D_RNN), const2),
            pl.BlockSpec((N_RG_BLOCKS, RG_BLOCK, RG_BLOCK), const3),
            pl.BlockSpec((1, D_RNN), const2),
            pl.BlockSpec((N_RG_BLOCKS, RG_BLOCK, RG_BLOCK), const3),
            pl.BlockSpec((1, D_RNN), const2),
            pl.BlockSpec((1, D_RNN), const2),
            pl.BlockSpec((D_RNN, d), const2, pipeline_mode=one),
        ],
        out_specs=[
            pl.BlockSpec((tm, d), lambda i: (i, 0)),
            pl.BlockSpec((1, D_RNN), const2),
            pl.BlockSpec((CONV_WIDTH - 1, D_RNN), const2),
        ],
        out_shape=[
            jax.ShapeDtypeStruct((m, d), F32),
            jax.ShapeDtypeStruct((1, D_RNN), F32),
            jax.ShapeDtypeStruct((CONV_WIDTH - 1, D_RNN), F32),
        ],
        scratch_shapes=[
            pltpu.VMEM((tm + SUBLANES, D_RNN), F32),
            pltpu.VMEM((tm, D_RNN), F32),
            pltpu.VMEM((tm, D_RNN), BF16),
            pltpu.VMEM((1, D_RNN), F32),
        ],
        compiler_params=_params("arbitrary"),
        name="rg_prompt",
    )(x, g, w_in, cw, cb, gaw, gab, gxw, gxb, lam, w_out)


def _rg_sample_kernel(x_ref, g_ref, win_ref, cw_ref, cb_ref, gaw_ref, gab_ref, gxw_ref, gxb_ref,
                      lam_ref, wout_ref, h0_ref, c0_ref, o_ref, hnew_ref, cnew_ref, y_ref):
    x = x_ref[...]
    xn = _rms(x, g_ref[...]).astype(BF16)
    gate = jnp.dot(xn, win_ref[:, 0:D_RNN], preferred_element_type=F32)
    rec = jnp.dot(xn, win_ref[:, D_RNN:2 * D_RNN], preferred_element_type=F32)
    for n in range(N_RG_BLOCKS):
        cs = slice(n * RG_BLOCK, (n + 1) * RG_BLOCK)
        taps = [c0_ref[:, j * D_RNN + n * RG_BLOCK:j * D_RNN + (n + 1) * RG_BLOCK]
                for j in range(CONV_WIDTH - 1)] + [rec[:, cs]]
        xc = cb_ref[:, cs] + taps[0] * cw_ref[0:1, cs]
        for j in range(1, CONV_WIDTH):
            xc = xc + taps[j] * cw_ref[j:j + 1, cs]
        sp = _softplus(-lam_ref[:, cs])
        a, b = _rg_gates(xc, gaw_ref[n], gab_ref[:, cs], gxw_ref[n], gxb_ref[:, cs], sp)
        h = a * h0_ref[:, cs] + b
        hnew_ref[:, cs] = h
        y_ref[:, cs] = (h * jax.nn.gelu(gate[:, cs])).astype(BF16)
    o_ref[...] = x + jnp.dot(y_ref[...], wout_ref[...], preferred_element_type=F32)
    cnew_ref[:, 0:2 * D_RNN] = c0_ref[:, D_RNN:3 * D_RNN]
    cnew_ref[:, 2 * D_RNN:3 * D_RNN] = rec


def _rg_sample(x, g, w_in, cw, cb, gaw, gab, gxw, gxb, lam, w_out, h0, c0):
    m, d = x.shape
    vm = pl.BlockSpec(memory_space=pltpu.VMEM)
    return pl.pallas_call(
        _rg_sample_kernel,
        in_specs=[vm] * 13,
        out_specs=[vm] * 3,
        out_shape=[
            jax.ShapeDtypeStruct((m, d), F32),
            jax.ShapeDtypeStruct((m, D_RNN), F32),
            jax.ShapeDtypeStruct((m, (CONV_WIDTH - 1) * D_RNN), F32),
        ],
        scratch_shapes=[pltpu.VMEM((m, D_RNN), BF16)],
        compiler_params=pltpu.CompilerParams(vmem_limit_bytes=VMEM_LIMIT_BYTES),
        name="rg_sample",
    )(x, g, w_in, cw, cb, gaw, gab, gxw, gxb, lam, w_out, h0, c0)


_PROJ_CHUNK = 512


def _proj_kernel(*refs, tm, dil, n_mats, rope, scale, pos0, pos_stride, out_dtype, has_tail):
    x_ref, g_ref, inv_ref = refs[:3]
    w_refs = refs[3:3 + n_mats]
    o_ref = refs[3 + n_mats]
    tail_ref = refs[4 + n_mats] if has_tail else None
    slab_ref, xn_ref = refs[-2:]
    i = pl.program_id(0)
    half = ROPE_DIM // 2
    th = tm

    for t0 in range(0, tm, th):
        rows = slice(t0, t0 + th)
        urows = slice(t0 // dil, (t0 + th) // dil)
        xn_ref[rows, :] = _rms(x_ref[rows, :], g_ref[...]).astype(BF16)
        if any(rope):
            r = lax.broadcasted_iota(jnp.int32, (th, LANES), 0)
            pos = pos0 + (i * tm + t0 + r) * pos_stride
            ang = pos.astype(F32) * inv_ref[...]
            lane = lax.broadcasted_iota(jnp.int32, (th, LANES), 1)
            sin = jnp.sin(ang)
            cos = jnp.cos(ang)
            s_lo = jnp.where(lane < half, -sin, 0.0)
            s_hi = jnp.where((lane >= half) & (lane < ROPE_DIM), sin, 0.0)

        for k in range(n_mats):
            for c0 in range(0, GW, _PROJ_CHUNK):
                acc = jnp.dot(xn_ref[rows, :], w_refs[k][:, c0:c0 + _PROJ_CHUNK],
                              preferred_element_type=F32)
                for c in range(c0, c0 + _PROJ_CHUNK, LANES):
                    a = acc[:, c - c0:c - c0 + LANES]
                    if rope[k]:
                        a = (a * cos + pltpu.roll(a, half, 1) * s_hi
                             + pltpu.roll(a, LANES - half, 1) * s_lo)
                    if scale[k] != 1.0:
                        a = a * scale[k]
                    if has_tail:
                        tail_ref[k, rows, c:c + LANES] = a
                    if dil == 1:
                        o_ref[k, 0, rows, c:c + LANES] = a.astype(out_dtype)
                    else:
                        s = c // LANES % slab_ref.shape[0]
                        slab_ref[s, 0:th, :] = a
                        for r_ in range(dil):
                            o_ref[k, r_, urows, c:c + LANES] = (
                                slab_ref[s, pl.ds(r_, th // dil, stride=dil), :].astype(out_dtype))


def _proj(x, g, inv, w, col_blocks, *, dil, rope, scale, pos0, pos_stride, tm, out_dtype, tail=0,
          shift=None):
    m, d = x.shape
    n_mats = len(col_blocks)
    assert m % tm == 0 and tm % dil == 0
    lrows = m // dil
    tail_rows = max(tail, tm) if tail else 0
    i0 = (m - tail_rows) // tm if tail else 0
    const2 = lambda i: (0, 0)
    in_specs = [pl.BlockSpec((tm, d), lambda i: (i, 0)), pl.BlockSpec((1, d), const2),
                pl.BlockSpec((1, LANES), const2)]
    for cb in col_blocks:
        in_specs.append(pl.BlockSpec((d, GW), lambda i, cb=cb: (0, cb), pipeline_mode=pl.Buffered(1)))
    out_specs = [pl.BlockSpec((n_mats, dil, tm // dil, GW), lambda i: (0, 0, i, 0))]
    out_shape = [jax.ShapeDtypeStruct((n_mats, dil, lrows, GW), out_dtype)]
    if tail:
        out_specs.append(pl.BlockSpec((n_mats, tm, GW), lambda i: (0, jnp.maximum(i - i0, 0), 0)))
        out_shape.append(jax.ShapeDtypeStruct((n_mats, tail_rows, GW), F32))
    outs = _call(
        functools.partial(_proj_kernel, tm=tm, dil=dil, n_mats=n_mats, rope=tuple(rope),
                          scale=tuple(scale), pos0=pos0, pos_stride=pos_stride,
                          out_dtype=out_dtype, has_tail=bool(tail)),
        name="norm_proj", grid=(m // tm,), in_specs=in_specs, args=[x, g, inv] + [w] * n_mats,
        out_specs=out_specs, out_shape=out_shape,
        scratch=[pltpu.VMEM((2, tm, LANES), F32), pltpu.VMEM((tm, d), BF16)],
        sem="arbitrary", shift=shift)
    return (outs[0], outs[1][:, tail_rows - tail:] if tail else None,
            outs[-1] if shift is not None else None)


_ATTN_HEADS_PER_ITER = 2


def _attn_kernel(q_ref, kc_ref, kp_ref, vc_ref, vp_ref, o_ref, lse_ref, *, tq, nb):
    ub = pl.program_id(0) % nb
    blk = LANES
    nsub = tq // blk
    row = lax.broadcasted_iota(jnp.int32, (blk, blk), 0)
    col = lax.broadcasted_iota(jnp.int32, (blk, blk), 1)
    lane = lax.broadcasted_iota(jnp.int32, (blk, LANES), 1)
    cur_ok = col <= row
    prev_tri = col >= row
    neg = -jnp.inf
    dn = (((1,), (1,)), ((), ()))

    first_ok = jnp.logical_and(prev_tri, ub > 0)
    lse_ref[...] = jnp.zeros_like(lse_ref)

    rows = [slice(c * blk, (c + 1) * blk) for c in range(nsub)]
    units = [(e, c) for e in range(_ATTN_HEADS_PER_ITER) for c in range(nsub)]

    def heads(it, carry):
        hn = [it * _ATTN_HEADS_PER_ITER + e for e in range(_ATTN_HEADS_PER_ITER)]
        hs = [pl.ds(pl.multiple_of(h * HEAD_DIM, HEAD_DIM), HEAD_DIM) for h in hn]
        ks = [[kp_ref[:, s]] + [kc_ref[rs, s] for rs in rows] for s in hs]
        vs = [[vp_ref[:, s]] + [vc_ref[rs, s] for rs in rows] for s in hs]
        scores = {}
        for e, c in units:
            q = q_ref[rows[c], hs[e]]
            prev_ok = prev_tri if c > 0 else first_ok
            s_p = jnp.where(prev_ok, lax.dot_general(q, ks[e][c], dn, preferred_element_type=F32), neg)
            s_c = jnp.where(cur_ok, lax.dot_general(q, ks[e][c + 1], dn, preferred_element_type=F32), neg)
            scores[e, c] = (s_p, s_c)
        probs = {}
        lse_new = [lse_ref[rs, :] for rs in rows]
        for e, c in units:
            s_p, s_c = scores[e, c]
            m = jnp.max(jnp.maximum(s_p, s_c), axis=1, keepdims=True)
            p_p = jnp.exp(s_p - m)
            p_c = jnp.exp(s_c - m)
            den = jnp.sum(p_p + p_c, axis=1, keepdims=True)
            lse_new[c] = jnp.where(lane == hn[e], m + jnp.log(den), lse_new[c])
            probs[e, c] = (p_p.astype(BF16), p_c.astype(BF16), 1.0 / den)
        for c, rs in enumerate(rows):
            lse_ref[rs, :] = lse_new[c]
        for e, c in units:
            p_p, p_c, rden = probs[e, c]
            o = (jnp.dot(p_p, vs[e][c], preferred_element_type=F32)
                 + jnp.dot(p_c, vs[e][c + 1], preferred_element_type=F32))
            o_ref[rows[c], hs[e]] = o * rden
        return carry

    lax.fori_loop(0, N_HEADS // _ATTN_HEADS_PER_ITER, heads, 0)


def _attn_group(q, kv, *, tq, shift=None):
    _, dil, lrows, _ = q.shape
    s = dil * lrows
    nb = lrows // tq
    sub = tq // LANES
    qv = q.reshape(s, GW)
    kvv = kv.reshape(2, s, GW)
    prev_idx = lambda i: jnp.maximum(i * sub - 1, 0)
    outs = _call(
        functools.partial(_attn_kernel, tq=tq, nb=nb),
        name="attn_group", grid=(s // tq,),
        in_specs=[
            pl.BlockSpec((tq, GW), lambda i: (i, 0)),
            pl.BlockSpec((None, tq, GW), lambda i: (0, i, 0)),
            pl.BlockSpec((None, LANES, GW), lambda i: (0, prev_idx(i), 0)),
            pl.BlockSpec((None, tq, GW), lambda i: (1, i, 0)),
            pl.BlockSpec((None, LANES, GW), lambda i: (1, prev_idx(i), 0)),
        ],
        args=[qv, kvv, kvv, kvv, kvv],
        out_specs=[
            pl.BlockSpec((tq, GW), lambda i: (i, 0)),
            pl.BlockSpec((tq, LANES), lambda i: (i, 0)),
        ],
        out_shape=[
            jax.ShapeDtypeStruct((s, GW), F32),
            jax.ShapeDtypeStruct((s, LANES), F32),
        ],
        scratch=[], sem="arbitrary", shift=shift)
    return (outs[0].reshape(dil, lrows, GW), outs[1].reshape(dil, lrows, LANES),
            outs[-1] if shift is not None else None)


def _merge_wo_kernel(x_ref, o1_ref, o2_ref, o3_ref, l1_ref, l2_ref, l3_ref, w_ref, out_ref,
                     onat_ref, lnat_ref, a_ref, *, tm):
    o_refs = (o1_ref, o2_ref, o3_ref)
    l_refs = (l1_ref, l2_ref, l3_ref)
    for g, dil in enumerate(DILATIONS):
        rows = tm // dil
        for r in range(dil):
            dst = pl.ds(r, rows, stride=dil) if dil > 1 else slice(None)
            lnat_ref[g, dst, :] = l_refs[g][r]
            for h in range(N_HEADS):
                onat_ref[g * N_HEADS + h, dst, :] = o_refs[g][r, :, h * HEAD_DIM:(h + 1) * HEAD_DIM]
    l1, l2, l3 = lnat_ref[0], lnat_ref[1], lnat_ref[2]
    mx = jnp.maximum(jnp.maximum(l1, l2), l3)
    e1, e2, e3 = jnp.exp(l1 - mx), jnp.exp(l2 - mx), jnp.exp(l3 - mx)
    tot = (e1 + e2) + e3
    w1, w2, w3 = e1 / tot, e2 / tot, e3 / tot
    for h in range(N_HEADS):
        a = onat_ref[h] * w1[:, h:h + 1]
        a = a + onat_ref[N_HEADS + h] * w2[:, h:h + 1]
        a = a + onat_ref[2 * N_HEADS + h] * w3[:, h:h + 1]
        a_ref[:, h * HEAD_DIM:(h + 1) * HEAD_DIM] = a.astype(BF16)
    out_ref[...] = x_ref[...] + jnp.dot(a_ref[...], w_ref[...], preferred_element_type=F32)


def _merge_wo(x, os_, ls_, w_o, *, tm, shift=None):
    m, d = x.shape
    rowblk = lambda i: (i, 0)
    deint = lambda i: (0, i, 0)
    in_specs = [pl.BlockSpec((tm, d), rowblk)]
    in_specs += [pl.BlockSpec((dil, tm // dil, GW), deint) for dil in DILATIONS]
    in_specs += [pl.BlockSpec((dil, tm // dil, LANES), deint) for dil in DILATIONS]
    in_specs += [pl.BlockSpec((GW, d), lambda i: (0, 0), pipeline_mode=pl.Buffered(1))]
    outs = _call(
        functools.partial(_merge_wo_kernel, tm=tm),
        name="merge_wo", grid=(m // tm,), in_specs=in_specs, args=[x, *os_, *ls_, w_o],
        out_specs=[pl.BlockSpec((tm, d), rowblk)],
        out_shape=[jax.ShapeDtypeStruct((m, d), F32)],
        scratch=[pltpu.VMEM((N_GROUPS * N_HEADS, tm, LANES), F32),
                 pltpu.VMEM((N_GROUPS, tm, LANES), F32),
                 pltpu.VMEM((tm, GW), BF16)],
        sem="arbitrary", shift=shift)
    return outs[0], (outs[-1] if shift is not None else None)


def _bf16_round(v):
    return v.astype(BF16).astype(F32)


def _attn_sample_kernel(q_ref, kvn_ref, c1_ref, c2_ref, c3_ref, o_ref):
    outs, lses = [], []
    for g, c_ref in enumerate((c1_ref, c2_ref, c3_ref)):
        q = _bf16_round(q_ref[g, 0])
        kn = _bf16_round(kvn_ref[g, 0, 0])
        vn = _bf16_round(kvn_ref[g, 1, 0])
        kc = _bf16_round(c_ref[0, :, 0, 0])
        vc = _bf16_round(c_ref[0, :, 0, 1])
        s_c = jnp.sum(kc * q[None], axis=-1, keepdims=True)
        s_n = jnp.sum(kn * q, axis=-1, keepdims=True)
        m = jnp.maximum(jnp.max(s_c, axis=0), s_n)
        p_c = jnp.exp(s_c - m[None])
        p_n = jnp.exp(s_n - m)
        den = jnp.sum(p_c, axis=0) + p_n
        o = jnp.sum(_bf16_round(p_c / den[None]) * vc, axis=0) + _bf16_round(p_n / den) * vn
        outs.append(o)
        lses.append(m + jnp.log(den))
    mx = jnp.maximum(jnp.maximum(lses[0], lses[1]), lses[2])
    es = [jnp.exp(l - mx) for l in lses]
    tot = (es[0] + es[1]) + es[2]
    acc = outs[0] * (es[0] / tot)
    acc = acc + outs[1] * (es[1] / tot)
    acc = acc + outs[2] * (es[2] / tot)
    o_ref[0] = acc


def _attn_sample(q, kvn, caches):
    b = q.shape[1]
    nk = N_KEYS - 1
    views = [c.reshape(b, nk, DILATIONS[g], 2, N_HEADS, HEAD_DIM) for g, c in enumerate(caches)]
    cspec = pl.BlockSpec((1, nk, 1, 2, N_HEADS, HEAD_DIM), lambda i: (i, 0, 0, 0, 0, 0))
    return pl.pallas_call(
        _attn_sample_kernel,
        grid=(b,),
        in_specs=[
            pl.BlockSpec((N_GROUPS, 1, N_HEADS, HEAD_DIM), lambda i: (0, i, 0, 0)),
            pl.BlockSpec((N_GROUPS, 2, 1, N_HEADS, HEAD_DIM), lambda i: (0, 0, i, 0, 0)),
            cspec, cspec, cspec,
        ],
        out_specs=pl.BlockSpec((1, N_HEADS, HEAD_DIM), lambda i: (i, 0, 0)),
        out_shape=jax.ShapeDtypeStruct((b, N_HEADS, HEAD_DIM), F32),
        compiler_params=_params("parallel"),
        name="attn_sample",
    )(q, kvn, *views)


def _matmul_res_kernel(x_ref, a_ref, w_ref, o_ref):
    o_ref[...] = x_ref[...] + jnp.dot(a_ref[...].astype(BF16), w_ref[...], preferred_element_type=F32)


def _matmul_res(x, a, w):
    vm = pl.BlockSpec(memory_space=pltpu.VMEM)
    return pl.pallas_call(
        _matmul_res_kernel,
        in_specs=[vm] * 3,
        out_specs=vm,
        out_shape=jax.ShapeDtypeStruct(x.shape, F32),
        compiler_params=pltpu.CompilerParams(vmem_limit_bytes=VMEM_LIMIT_BYTES),
        name="matmul_res",
    )(x, a, w)


def kernel(x_prompt, x_sample, state_h, state_conv, cache_kv_g1, cache_kv_g2, cache_kv_g3,
           ffn1_norm, ffn1_w_up, ffn1_w_down, mix_norm, ffn2_norm, ffn2_w_up, ffn2_w_down,
           rg_w_in, rg_conv_w, rg_conv_b, rg_gate_a_w, rg_gate_a_b, rg_gate_x_w, rg_gate_x_b,
           rg_lambda, rg_w_out, kv_norm, w_kv, attn_w_q, attn_w_o, final_norm):
    caches = [cache_kv_g1, cache_kv_g2, cache_kv_g3]
    bp, sp, d = x_prompt.shape
    db, ds, _ = x_sample.shape
    assert bp == 1 and ds == 1 and d == D_MODEL
    assert all(c.shape[1] == w for c, w in zip(caches, WINDOWS))
    past = PAST_LEN

    xp = x_prompt.reshape(sp, d)
    xs = x_sample.reshape(db, d)
    row = lambda v: v.reshape(1, -1)
    bf = lambda w: w.astype(BF16)

    half = ROPE_DIM // 2
    inv = ROPE_THETA ** (-jnp.arange(half, dtype=F32) / half)
    inv_lane = jnp.concatenate([inv, inv, jnp.zeros((LANES - ROPE_DIM,), F32)]).reshape(1, LANES)
    fg = row(final_norm)

    def ffn_pair(xp, xs, g, w_up, w_down, layer, final=False, shift=None):
        xs, w_bf, _ = _ffn(xs, row(g[layer]), (w_up, w_down), fg, layer=layer, final=final, tm=db,
                           tf=512)
        xp, _, shifted = _ffn(xp, row(g[layer]), w_bf, fg, layer=layer, final=final, tm=512, tf=512,
                              shift=shift)
        return xp, xs, shifted

    t12 = 128
    whole = lambda g: _ShiftJob(caches[g], None, 0, db * (WINDOWS[g] // t12), t12)
    t3 = 256
    nblk3 = db * (WINDOWS[2] // t3)
    shifted3, next_blk = None, 0

    def ride(steps):
        nonlocal next_blk
        count = min(steps, nblk3 - next_blk)
        job = _ShiftJob(caches[2], shifted3, next_blk, count, t3)
        next_blk += count
        return job

    xp, xs, shifted2 = ffn_pair(xp, xs, ffn1_norm, ffn1_w_up, ffn1_w_down, 0, shift=whole(1))
    rg = (row(mix_norm[0]), bf(rg_w_in[0]), rg_conv_w[0], row(rg_conv_b[0]), bf(rg_gate_a_w[0]),
          row(rg_gate_a_b[0]), bf(rg_gate_x_w[0]), row(rg_gate_x_b[0]), row(rg_lambda[0]),
          bf(rg_w_out[0]))
    xp, p_h, p_conv = _rg_prompt(xp, *rg, tm=256)
    xs, s_h, s_conv = _rg_sample(xs, *rg, state_h[0], state_conv[0].reshape(db, -1))
    xp, xs, shifted1 = ffn_pair(xp, xs, ffn2_norm, ffn2_w_up, ffn2_w_down, 0, shift=whole(0))

    wkv = bf(w_kv)
    kvg = row(kv_norm)
    tm_p = 256
    kv_p, p_kv, kv_s = [], [], []
    for g in range(N_GROUPS):
        common = dict(rope=(True, False), scale=(1.0, 1.0))
        kv, tail, shifted3 = _proj(xp, kvg, inv_lane, wkv, (g, N_GROUPS + g), dil=DILATIONS[g],
                                   pos0=0, pos_stride=1, tm=tm_p, out_dtype=BF16, tail=WINDOWS[g],
                                   shift=ride(sp // tm_p), **common)
        kv_p.append(kv)
        p_kv.append(jnp.swapaxes(tail, 0, 1).reshape(1, WINDOWS[g], 2, N_HEADS, HEAD_DIM))
        kv_s.append(_proj(xs, kvg, inv_lane, wkv, (g, N_GROUPS + g), dil=1, pos0=past,
                          pos_stride=0, tm=db, out_dtype=F32, **common)[0])
    kvn = jnp.stack(kv_s).reshape(N_GROUPS, 2, db, N_HEADS, HEAD_DIM)

    xp, xs, _ = ffn_pair(xp, xs, ffn1_norm, ffn1_w_up, ffn1_w_down, 1)
    wq = bf(attn_w_q[0])
    wo = bf(attn_w_o[0])
    qg = row(mix_norm[1])
    qcommon = dict(rope=(True,), scale=(HEAD_DIM ** -0.5,))
    tq = 512
    og, lg, q_s = [], [], []
    for g in range(N_GROUPS):
        q, _, shifted3 = _proj(xp, qg, inv_lane, wq, (g,), dil=DILATIONS[g], pos0=0, pos_stride=1,
                               tm=tm_p, out_dtype=BF16, shift=ride(sp // tm_p), **qcommon)
        o, lse, shifted3 = _attn_group(q, kv_p[g], tq=tq, shift=ride(sp // tq))
        og.append(o)
        lg.append(lse)
        q_s.append(_proj(xs, qg, inv_lane, wq, (g,), dil=1, pos0=past, pos_stride=0, tm=db,
                         out_dtype=F32, **qcommon)[0])
    xp, shifted3 = _merge_wo(xp, og, lg, wo, tm=tm_p, shift=ride(sp // tm_p))
    assert next_blk == nblk3
    qs = jnp.stack(q_s).reshape(N_GROUPS, db, N_HEADS, HEAD_DIM)
    xs = _matmul_res(xs, _attn_sample(qs, kvn, caches).reshape(db, GW), wo)
    xp, xs, _ = ffn_pair(xp, xs, ffn2_norm, ffn2_w_up, ffn2_w_down, 1, final=True)

    s_kv = _insert_rows([kvn[g] for g in range(N_GROUPS)], [shifted1, shifted2, shifted3])

    return (xp.reshape(bp, sp, d), xs.reshape(db, ds, d),
            p_h.reshape(1, bp, D_RNN), p_conv.reshape(1, bp, CONV_WIDTH - 1, D_RNN),
            p_kv[0], p_kv[1], p_kv[2],
            s_h.reshape(1, db, D_RNN), s_conv.reshape(1, db, CONV_WIDTH - 1, D_RNN),
            s_kv[0], s_kv[1], s_kv[2])
```

```python
import functools
from typing import NamedTuple, Optional

import jax
import jax.numpy as jnp
from jax import lax
from jax.experimental import pallas as pl
from jax.experimental.pallas import tpu as pltpu

F32 = jnp.float32
BF16 = jnp.bfloat16

D_MODEL = 2048
D_FF = 5632
D_RNN = 2048
N_RG_BLOCKS = 16
RG_BLOCK = D_RNN // N_RG_BLOCKS
CONV_WIDTH = 4
RG_C = 8.0
HEAD_DIM = 128
N_HEADS = 16
N_GROUPS = 3
WINDOWS = (128, 512, 2048)
DILATIONS = (1, 4, 16)
N_KEYS = 129
ROPE_DIM = 32
ROPE_THETA = 500000.0
EPS = 1e-6
PAST_LEN = 8192
GW =N_HEADS * HEAD_DIM

LANES = 128
SUBLANES = 8
VMEM_LIMIT_BYTES = 60 * 1024 * 1024


def _params(*sem):
    return pltpu.CompilerParams(dimension_semantics=sem, vmem_limit_bytes=VMEM_LIMIT_BYTES)


def _rms(x, g):
    var = jnp.mean(x * x, axis=-1, keepdims=True)
    return (x * lax.rsqrt(var + EPS)) * g


def _softplus(z):
    return jnp.maximum(z, 0.0) + jnp.log1p(jnp.exp(-jnp.abs(z)))


def _sigmoid(z):
    return 0.5 * jnp.tanh(0.5 * z) + 0.5


class _ShiftJob(NamedTuple):
    cache: jax.Array
    prev: Optional[jax.Array]
    first: int
    count: int
    t: int


class _Rider:
    def __init__(self, c_ref, nxt_ref, o_ref):
        self.c, self.nxt, self.o = c_ref, nxt_ref, o_ref
        self.t = c_ref.shape[1]

    def part(self, p, n):
        rp = self.t // n
        lo, hi = p * rp, (p + 1) * rp
        if p == n - 1:
            self.o[0, lo:hi - 1] = self.c[0, lo + 1:hi]
            self.o[0, hi - 1] = self.nxt[0, 0]
        else:
            self.o[0, lo:hi] = self.c[0, lo + 1:hi + 1]

    def loop_part(self, it, n):
        rp = self.t // n
        lo = jnp.minimum(it, n - 2) * rp
        self.o[0, pl.ds(lo, rp)] = self.c[0, pl.ds(lo + 1, rp)]


def _ride(rider, p, n):
    if rider is not None:
        rider.part(p, n)


def _with_shift(kernel_fn, n_in, n_out, n_shift_in):
    def wrapped(*refs):
        outs_at = n_in + n_shift_in
        rider = _Rider(refs[n_in], refs[n_in + 1], refs[outs_at + n_out])
        kernel_fn(*refs[:n_in], *refs[outs_at:outs_at + n_out], *refs[outs_at + n_out + 1:],
                  rider=rider)
    return wrapped


def _call(kernel_fn, *, name, grid, in_specs, args, out_specs, out_shape, scratch, sem, shift=None):
    aliases = {}
    if shift is not None:
        b, w = shift.cache.shape[:2]
        nblk = w // shift.t
        tail = shift.cache.shape[2:]
        zeros = (0,) * len(tail)

        def blk(*idx):
            step = idx[0]
            for extent, i in zip(grid[1:], idx[1:]):
                step = step * extent + i
            k = shift.first + jnp.minimum(step, shift.count - 1)
            return k // nblk, k % nblk

        def nxt(*idx):
            bi, ji = blk(*idx)
            return bi, jnp.minimum((ji + 1) * shift.t, w - 1)

        shift_in = [pl.BlockSpec((1, shift.t) + tail, lambda *idx: blk(*idx) + zeros),
                    pl.BlockSpec((1, 1) + tail, lambda *idx: nxt(*idx) + zeros)]
        shift_args = [shift.cache, shift.cache]
        if shift.prev is not None:
            shift_in.append(pl.BlockSpec(memory_space=pl.ANY))
            shift_args.append(shift.prev)
            aliases = {len(in_specs) + 2: len(out_specs)}
        kernel_fn = _with_shift(kernel_fn, len(in_specs), len(out_specs), len(shift_in))
        in_specs = list(in_specs) + shift_in
        args = list(args) + shift_args
        out_specs = list(out_specs) + [
            pl.BlockSpec((1, shift.t) + tail, lambda *idx: blk(*idx) + zeros)]
        out_shape = list(out_shape) + [jax.ShapeDtypeStruct(shift.cache.shape, shift.cache.dtype)]
    sems = (sem,) if isinstance(sem, str) else sem
    return pl.pallas_call(
        kernel_fn, grid=grid, in_specs=in_specs, out_specs=out_specs, out_shape=out_shape,
        input_output_aliases=aliases, scratch_shapes=scratch, compiler_params=_params(*sems),
        name=name,
    )(*args)


def _insert_rows_kernel(*refs):
    n = len(refs) // 3
    for new_ref, o_ref in zip(refs[:n], refs[2 * n:]):
        o_ref[0, 0] = new_ref[:, 0]


def _insert_rows(kv_new, shifted):
    n = len(shifted)
    b = shifted[0].shape[0]
    tail = shifted[0].shape[2:]
    row_spec = lambda w: pl.BlockSpec((1, 1) + tail, lambda i: (i, w - 1, 0, 0, 0))
    return pl.pallas_call(
        _insert_rows_kernel,
        grid=(b,),
        in_specs=[pl.BlockSpec((2, 1) + tail[1:], lambda i: (0, i, 0, 0))] * n
        + [pl.BlockSpec(memory_space=pl.ANY)] * n,
        out_specs=[row_spec(s.shape[1]) for s in shifted],
        out_shape=[jax.ShapeDtypeStruct(s.shape, s.dtype) for s in shifted],
        input_output_aliases={n + k: k for k in range(n)},
        compiler_params=_params("arbitrary"),
        name="insert_rows",
    )(*kv_new, *shifted)


def _ffn_kernel(x_ref, g_ref, wg_ref, wu_ref, wd_ref, fg_ref, o_ref, *rest, nf, final, emit,
                norm_next, rider=None):
    xn_ref = rest[-1]
    j = pl.program_id(1)

    @pl.when(j == 0)
    def _():
        xn_ref[...] = _rms(x_ref[...], g_ref[...]).astype(BF16)
        o_ref[...] = jnp.zeros_like(o_ref)

    wg, wu, wd = wg_ref[...], wu_ref[...], wd_ref[...]
    if emit:
        wg, wu, wd = wg.astype(BF16), wu.astype(BF16), wd.astype(BF16)
        rest[0][...], rest[1][...], rest[2][...] = wg, wu, wd

    xn = xn_ref[...]
    _ride(rider, 0, 4)
    gate = jnp.dot(xn, wg, preferred_element_type=F32)
    _ride(rider, 1, 4)
    up = jnp.dot(xn, wu, preferred_element_type=F32)
    _ride(rider, 2, 4)
    h = (gate * jax.nn.sigmoid(gate)) * up
    _ride(rider, 3, 4)
    o_ref[...] += jnp.dot(h.astype(BF16), wd, preferred_element_type=F32)

    @pl.when(j == nf - 1)
    def _():
        y = x_ref[...] + 0.5 * o_ref[...]
        if final:
            y = _rms(y, fg_ref[...])
        o_ref[...] = y
        if norm_next:
            rest[-2][...] = _rms(y, fg_ref[...]).astype(BF16)


def _ffn(x, g, weights, fg, *, layer, final, tm, tf, norm_next=False, shift=None):
    m, d = x.shape
    assert not (final and norm_next)
    emit = len(weights) == 2
    nf = D_FF // tf
    if emit:
        w_up, w_down = weights
        w_args = [w_up, w_up, w_down]
        w_specs = [
            pl.BlockSpec((None, d, tf), lambda i, j: (layer, 0, j)),
            pl.BlockSpec((None, d, tf), lambda i, j: (layer, 0, j + nf)),
            pl.BlockSpec((None, tf, d), lambda i, j: (layer, j, 0)),
        ]
    else:
        w_args = list(weights)
        w_specs = [None] * 3
    cast_specs = [
        pl.BlockSpec((d, tf), lambda i, j: (0, j)),
        pl.BlockSpec((d, tf), lambda i, j: (0, j)),
        pl.BlockSpec((tf, d), lambda i, j: (j, 0)),
    ]
    cast_shapes = [jax.ShapeDtypeStruct((d, D_FF), BF16)] * 2 + [jax.ShapeDtypeStruct((D_FF, d), BF16)]
    row_spec = pl.BlockSpec((tm, d), lambda i, j: (i, 0))
    extra_specs = (cast_specs if emit else []) + ([row_spec] if norm_next else [])
    extra_shapes = (cast_shapes if emit else []) + (
        [jax.ShapeDtypeStruct((m, d), BF16)] if norm_next else [])
    outs = _call(
        functools.partial(_ffn_kernel, nf=nf, final=final, emit=emit, norm_next=norm_next),
        name="ffn", grid=(m // tm, nf),
        in_specs=[
            pl.BlockSpec((tm, d), lambda i, j: (i, 0)),
            pl.BlockSpec((1, d), lambda i, j: (0, 0)),
            *(w_specs if emit else cast_specs),
            pl.BlockSpec((1, d), lambda i, j: (0, 0)),
        ],
        args=[x, g, *w_args, fg],
        out_specs=[row_spec] + extra_specs,
        out_shape=[jax.ShapeDtypeStruct((m, d), F32)] + extra_shapes,
        scratch=[pltpu.VMEM((tm, d), BF16)], sem=("arbitrary", "arbitrary"), shift=shift)
    n_cast = 3 if emit else 0
    return (outs[0], tuple(outs[1:4]) if emit else None,
            outs[1 + n_cast] if norm_next else None, outs[-1] if shift is not None else None)


def _rg_gates(xc, gaw, gab, gxw, gxb, sp):
    xcb = xc.astype(BF16)
    r = _sigmoid(jnp.dot(xcb, gaw, preferred_element_type=F32) + gab)
    ig = _sigmoid(jnp.dot(xcb, gxw, preferred_element_type=F32) + gxb)
    log_a = (-RG_C * r) * sp
    a = jnp.exp(log_a)
    b = jnp.sqrt(-jnp.tanh(log_a) * (1.0 + a * a)) * (ig * xc)
    return a, b


def _rg_prompt_kernel(x_ref, g_ref, win_ref, cw_ref, cb_ref, gaw_ref, gab_ref, gxw_ref, gxb_ref,
                      lam_ref, wout_ref, o_ref, hlast_ref, conv_ref,
                      ext_ref, gate_ref, y_ref, h_ref, *, tm):
    i = pl.program_id(0)
    pad = SUBLANES

    @pl.when(i == 0)
    def _():
        ext_ref[0:pad, :] = jnp.zeros((pad, D_RNN), F32)
        h_ref[...] = jnp.zeros_like(h_ref)

    x = x_ref[...]
    y_ref[...] = _rms(x, g_ref[...]).astype(BF16)
    gate_ref[...] = jnp.dot(y_ref[...], win_ref[:, 0:D_RNN], preferred_element_type=F32)
    ext_ref[pad:pad + tm, :] = jnp.dot(y_ref[...], win_ref[:, D_RNN:2 * D_RNN],
                                       preferred_element_type=F32)

    sub = lax.broadcasted_iota(jnp.int32, (tm // SUBLANES, SUBLANES, RG_BLOCK), 1)
    for n in range(N_RG_BLOCKS):
        cs = slice(n * RG_BLOCK, (n + 1) * RG_BLOCK)
        xc = cb_ref[:, cs] + ext_ref[pad - 3:pad - 3 + tm, cs] * cw_ref[0:1, cs]
        for j in range(1, CONV_WIDTH):
            xc = xc + ext_ref[pad - 3 + j:pad - 3 + j + tm, cs] * cw_ref[j:j + 1, cs]
        sp = _softplus(-lam_ref[:, cs])
        a, b = _rg_gates(xc, gaw_ref[n], gab_ref[:, cs], gxw_ref[n], gxb_ref[:, cs], sp)
        a = a.reshape(tm // SUBLANES, SUBLANES, RG_BLOCK)
        b = b.reshape(tm // SUBLANES, SUBLANES, RG_BLOCK)
        s = 1
        while s < SUBLANES:
            keep = sub >= s
            b = b + a * jnp.where(keep, pltpu.roll(b, s, 1), 0.0)
            a = a * jnp.where(keep, pltpu.roll(a, s, 1), 1.0)
            s *= 2
        carry = jnp.broadcast_to(h_ref[:, cs], (SUBLANES, RG_BLOCK))
        hs = []
        for k in range(tm // SUBLANES):
            hk = a[k] * carry + b[k]
            hs.append(hk)
            carry = jnp.broadcast_to(hk[SUBLANES - 1:SUBLANES, :], (SUBLANES, RG_BLOCK))
        h = jnp.concatenate(hs, axis=0)
        h_ref[:, cs] = carry[0:1, :]
        y_ref[:, cs] = (h * jax.nn.gelu(gate_ref[:, cs])).astype(BF16)

    o_ref[...] = x + jnp.dot(y_ref[...], wout_ref[...], preferred_element_type=F32)
    hlast_ref[...] = h_ref[...]
    conv_ref[...] = ext_ref[pad + tm - 3:pad + tm, :]
    ext_ref[0:pad, :] = ext_ref[tm:tm + pad, :]


def _rg_prompt(x, g, w_in, cw, cb, gaw, gab, gxw, gxb, lam, w_out, *, tm):
    m, d = x.shape
    const2 = lambda i: (0, 0)
    const3 = lambda i: (0, 0, 0)
    one = pl.Buffered(1)
    return pl.pallas_call(
        functools.partial(_rg_prompt_kernel, tm=tm),
        grid=(m // tm,),
        in_specs=[
            pl.BlockSpec((tm, d), lambda i: (i, 0)),
            pl.BlockSpec((1, d), const2),
            pl.BlockSpec((d, 2 * D_RNN), const2, pipeline_mode=one),
            pl.BlockSpec((CONV_WIDTH, D_RNN), const2),
            pl.BlockSpec((1, D_RNN), const2),
            pl.BlockSpec((N_RG_BLOCKS, RG_BLOCK, RG_BLOCK), const3),
            pl.BlockSpec((1, D_RNN), const2),
            pl.BlockSpec((N_RG_BLOCKS, RG_BLOCK, RG_BLOCK), const3),
            pl.BlockSpec((1, D_RNN), const2),
            pl.BlockSpec((1, D_RNN), const2),
            pl.BlockSpec((D_RNN, d), const2, pipeline_mode=one),
        ],
        out_specs=[
            pl.BlockSpec((tm, d), lambda i: (i, 0)),
            pl.BlockSpec((1, D_RNN), const2),
            pl.BlockSpec((CONV_WIDTH - 1, D_RNN), const2),
        ],
        out_shape=[
            jax.ShapeDtypeStruct((m, d), F32),
            jax.ShapeDtypeStruct((1, D_RNN), F32),
            jax.ShapeDtypeStruct((CONV_WIDTH - 1, D_RNN), F32),
        ],
        scratch_shapes=[
            pltpu.VMEM((tm + SUBLANES, D_RNN), F32),
            pltpu.VMEM((tm, D_RNN), F32),
            pltpu.VMEM((tm, D_RNN), BF16),
            pltpu.VMEM((1, D_RNN), F32),
        ],
        compiler_params=_params("arbitrary"),
        name="rg_prompt",
    )(x, g, w_in, cw, cb, gaw, gab, gxw, gxb, lam, w_out)


def _rg_sample_kernel(x_ref, g_ref, win_ref, cw_ref, cb_ref, gaw_ref, gab_ref, gxw_ref, gxb_ref,
                      lam_ref, wout_ref, h0_ref, c0_ref, o_ref, hnew_ref, cnew_ref, y_ref):
    x = x_ref[...]
    xn = _rms(x, g_ref[...]).astype(BF16)
    gate = jnp.dot(xn, win_ref[:, 0:D_RNN], preferred_element_type=F32)
    rec = jnp.dot(xn, win_ref[:, D_RNN:2 * D_RNN], preferred_element_type=F32)
    for n in range(N_RG_BLOCKS):
        cs = slice(n * RG_BLOCK, (n + 1) * RG_BLOCK)
        taps = [c0_ref[:, j * D_RNN + n * RG_BLOCK:j * D_RNN + (n + 1) * RG_BLOCK]
                for j in range(CONV_WIDTH - 1)] + [rec[:, cs]]
        xc = cb_ref[:, cs] + taps[0] * cw_ref[0:1, cs]
        for j in range(1, CONV_WIDTH):
            xc = xc + taps[j] * cw_ref[j:j + 1, cs]
        sp = _softplus(-lam_ref[:, cs])
        a, b = _rg_gates(xc, gaw_ref[n], gab_ref[:, cs], gxw_ref[n], gxb_ref[:, cs], sp)
        h = a * h0_ref[:, cs] + b
        hnew_ref[:, cs] = h
        y_ref[:, cs] = (h * jax.nn.gelu(gate[:, cs])).astype(BF16)
    o_ref[...] = x + jnp.dot(y_ref[...], wout_ref[...], preferred_element_type=F32)
    cnew_ref[:, 0:2 * D_RNN] = c0_ref[:, D_RNN:3 * D_RNN]
    cnew_ref[:, 2 * D_RNN:3 * D_RNN] = rec


def _rg_sample(x, g, w_in, cw, cb, gaw, gab, gxw, gxb, lam, w_out, h0, c0):
    m, d = x.shape
    vm = pl.BlockSpec(memory_space=pltpu.VMEM)
    return pl.pallas_call(
        _rg_sample_kernel,
        in_specs=[vm] * 13,
        out_specs=[vm] * 3,
        out_shape=[
            jax.ShapeDtypeStruct((m, d), F32),
            jax.ShapeDtypeStruct((m, D_RNN), F32),
            jax.ShapeDtypeStruct((m, (CONV_WIDTH - 1) * D_RNN), F32),
        ],
        scratch_shapes=[pltpu.VMEM((m, D_RNN), BF16)],
        compiler_params=pltpu.CompilerParams(vmem_limit_bytes=VMEM_LIMIT_BYTES),
        name="rg_sample",
    )(x, g, w_in, cw, cb, gaw, gab, gxw, gxb, lam, w_out, h0, c0)


_PROJ_CHUNK = 512


def _rope_kernel(inv_ref, cos_ref, shi_ref, slo_ref, *, tr, pos0, pos_stride):
    half = ROPE_DIM // 2
    r = lax.broadcasted_iota(jnp.int32, (tr, LANES), 0)
    pos = pos0 + (pl.program_id(0) * tr + r) * pos_stride
    ang = pos.astype(F32) * inv_ref[...]
    lane = lax.broadcasted_iota(jnp.int32, (tr, LANES), 1)
    sin = jnp.sin(ang)
    cos_ref[...] = jnp.cos(ang)
    shi_ref[...] = jnp.where((lane >= half) & (lane < ROPE_DIM), sin, 0.0)
    slo_ref[...] = jnp.where(lane < half, -sin, 0.0)


def _rope_tables(inv_lane, n_rows, *, pos0, pos_stride):
    tr = min(n_rows, 1024)
    spec = pl.BlockSpec((tr, LANES), lambda i: (i, 0))
    return pl.pallas_call(
        functools.partial(_rope_kernel, tr=tr, pos0=pos0, pos_stride=pos_stride),
        grid=(n_rows // tr,),
        in_specs=[pl.BlockSpec((1, LANES), lambda i: (0, 0))],
        out_specs=[spec] * 3,
        out_shape=[jax.ShapeDtypeStruct((n_rows, LANES), F32)] * 3,
        compiler_params=_params("parallel"),
        name="rope_tables",
    )(inv_lane)


def _proj_kernel(*refs, tm, dil, n_mats, rope, scale, out_dtype, has_tail, rider=None):
    xn_ref, cos_ref, shi_ref, slo_ref = refs[:4]
    w_refs = refs[4:4 + n_mats]
    o_ref = refs[4 + n_mats]
    tail_ref = refs[5 + n_mats] if has_tail else None
    slab_ref = refs[-1]
    half = ROPE_DIM // 2
    th = tm

    for t0 in range(0, tm, th):
        rows = slice(t0, t0 + th)
        urows = slice(t0 // dil, (t0 + th) // dil)
        cos, s_hi, s_lo = cos_ref[rows, :], shi_ref[rows, :], slo_ref[rows, :]

        n_chunks = GW // _PROJ_CHUNK
        for k in range(n_mats):
            for c0 in range(0, GW, _PROJ_CHUNK):
                _ride(rider, k * n_chunks + c0 // _PROJ_CHUNK, n_mats * n_chunks)
                acc = jnp.dot(xn_ref[rows, :], w_refs[k][:, c0:c0 + _PROJ_CHUNK],
                              preferred_element_type=F32)
                for c in range(c0, c0 + _PROJ_CHUNK, LANES):
                    a = acc[:, c - c0:c - c0 + LANES]
                    if rope[k]:
                        a = (a * cos + pltpu.roll(a, half, 1) * s_hi
                             + pltpu.roll(a, LANES - half, 1) * s_lo)
                    if scale[k] != 1.0:
                        a = a * scale[k]
                    if has_tail:
                        tail_ref[k, rows, c:c + LANES] = a
                    if dil == 1:
                        o_ref[k, 0, rows, c:c + LANES] = a.astype(out_dtype)
                    else:
                        s = c // LANES % slab_ref.shape[0]
                        slab_ref[s, 0:th, :] = a
                        for r_ in range(dil):
                            o_ref[k, r_, urows, c:c + LANES] = (
                                slab_ref[s, pl.ds(r_, th // dil, stride=dil), :].astype(out_dtype))


def _proj(xn, tables, w, col_blocks, *, dil, rope, scale, tm, out_dtype, tail=0, shift=None):
    m, d = xn.shape
    n_mats = len(col_blocks)
    assert m % tm == 0 and tm % dil == 0
    lrows = m // dil
    tail_rows = max(tail, tm) if tail else 0
    i0 = (m - tail_rows) // tm if tail else 0
    in_specs = [pl.BlockSpec((tm, d), lambda i: (i, 0))] + [
        pl.BlockSpec((tm, LANES), lambda i: (i, 0))] * 3
    for cb in col_blocks:
        in_specs.append(pl.BlockSpec((d, GW), lambda i, cb=cb: (0, cb), pipeline_mode=pl.Buffered(1)))
    out_specs = [pl.BlockSpec((n_mats, dil, tm // dil, GW), lambda i: (0, 0, i, 0))]
    out_shape = [jax.ShapeDtypeStruct((n_mats, dil, lrows, GW), out_dtype)]
    if tail:
        out_specs.append(pl.BlockSpec((n_mats, tm, GW), lambda i: (0, jnp.maximum(i - i0, 0), 0)))
        out_shape.append(jax.ShapeDtypeStruct((n_mats, tail_rows, GW), F32))
    outs = _call(
        functools.partial(_proj_kernel, tm=tm, dil=dil, n_mats=n_mats, rope=tuple(rope),
                          scale=tuple(scale), out_dtype=out_dtype, has_tail=bool(tail)),
        name="proj", grid=(m // tm,), in_specs=in_specs, args=[xn, *tables] + [w] * n_mats,
        out_specs=out_specs, out_shape=out_shape, scratch=[pltpu.VMEM((2, tm, LANES), F32)],
        sem="arbitrary", shift=shift)
    return (outs[0], outs[1][:, tail_rows - tail:] if tail else None,
            outs[-1] if shift is not None else None)


_ATTN_HEADS_PER_ITER = 2


def _attn_kernel(q_ref, kc_ref, kp_ref, vc_ref, vp_ref, o_ref, lse_ref, *, tq, nb, rider=None):
    ub = pl.program_id(0) % nb
    blk = LANES
    nsub = tq // blk
    row = lax.broadcasted_iota(jnp.int32, (blk, blk), 0)
    col = lax.broadcasted_iota(jnp.int32, (blk, blk), 1)
    lane = lax.broadcasted_iota(jnp.int32, (blk, LANES), 1)
    cur_ok = col <= row
    prev_tri = col >= row
    neg = -jnp.inf
    dn = (((1,), (1,)), ((), ()))

    first_ok = jnp.logical_and(prev_tri, ub > 0)
    lse_ref[...] = jnp.zeros_like(lse_ref)

    rows = [slice(c * blk, (c + 1) * blk) for c in range(nsub)]
    units = [(e, c) for e in range(_ATTN_HEADS_PER_ITER) for c in range(nsub)]

    n_iter = N_HEADS // _ATTN_HEADS_PER_ITER

    def heads(it, carry):
        if rider is not None:
            rider.loop_part(it, n_iter)
        hn = [it * _ATTN_HEADS_PER_ITER + e for e in range(_ATTN_HEADS_PER_ITER)]
        hs = [pl.ds(pl.multiple_of(h * HEAD_DIM, HEAD_DIM), HEAD_DIM) for h in hn]
        ks = [[kp_ref[:, s]] + [kc_ref[rs, s] for rs in rows] for s in hs]
        vs = [[vp_ref[:, s]] + [vc_ref[rs, s] for rs in rows] for s in hs]
        scores = {}
        for e, c in units:
            q = q_ref[rows[c], hs[e]]
            prev_ok = prev_tri if c > 0 else first_ok
            s_p = jnp.where(prev_ok, lax.dot_general(q, ks[e][c], dn, preferred_element_type=F32), neg)
            s_c = jnp.where(cur_ok, lax.dot_general(q, ks[e][c + 1], dn, preferred_element_type=F32), neg)
            scores[e, c] = (s_p, s_c)
        probs = {}
        lse_new = [lse_ref[rs, :] for rs in rows]
        for e, c in units:
            s_p, s_c = scores[e, c]
            m = jnp.max(jnp.maximum(s_p, s_c), axis=1, keepdims=True)
            p_p = jnp.exp(s_p - m)
            p_c = jnp.exp(s_c - m)
            den = jnp.sum(p_p + p_c, axis=1, keepdims=True)
            lse_new[c] = jnp.where(lane == hn[e], m + jnp.log(den), lse_new[c])
            probs[e, c] = (p_p.astype(BF16), p_c.astype(BF16), 1.0 / den)
        for c, rs in enumerate(rows):
            lse_ref[rs, :] = lse_new[c]
        for e, c in units:
            p_p, p_c, rden = probs[e, c]
            o = (jnp.dot(p_p, vs[e][c], preferred_element_type=F32)
                 + jnp.dot(p_c, vs[e][c + 1], preferred_element_type=F32))
            o_ref[rows[c], hs[e]] = o * rden
        return carry

    lax.fori_loop(0, n_iter, heads, 0)
    _ride(rider, n_iter - 1, n_iter)


def _attn_group(q, kv, *, tq, shift=None):
    _, dil, lrows, _ = q.shape
    s = dil * lrows
    nb = lrows // tq
    sub = tq // LANES
    qv = q.reshape(s, GW)
    kvv = kv.reshape(2, s, GW)
    prev_idx = lambda i: jnp.maximum(i * sub - 1, 0)
    outs = _call(
        functools.partial(_attn_kernel, tq=tq, nb=nb),
        name="attn_group", grid=(s // tq,),
        in_specs=[
            pl.BlockSpec((tq, GW), lambda i: (i, 0)),
            pl.BlockSpec((None, tq, GW), lambda i: (0, i, 0)),
            pl.BlockSpec((None, LANES, GW), lambda i: (0, prev_idx(i), 0)),
            pl.BlockSpec((None, tq, GW), lambda i: (1, i, 0)),
            pl.BlockSpec((None, LANES, GW), lambda i: (1, prev_idx(i), 0)),
        ],
        args=[qv, kvv, kvv, kvv, kvv],
        out_specs=[
            pl.BlockSpec((tq, GW), lambda i: (i, 0)),
            pl.BlockSpec((tq, LANES), lambda i: (i, 0)),
        ],
        out_shape=[
            jax.ShapeDtypeStruct((s, GW), F32),
            jax.ShapeDtypeStruct((s, LANES), F32),
        ],
        scratch=[], sem="arbitrary", shift=shift)
    return (outs[0].reshape(dil, lrows, GW), outs[1].reshape(dil, lrows, LANES),
            outs[-1] if shift is not None else None)


def _merge_wo_kernel(x_ref, o1_ref, o2_ref, o3_ref, l1_ref, l2_ref, l3_ref, w_ref, out_ref,
                     onat_ref, lnat_ref, a_ref, *, tm, rider=None):
    o_refs = (o1_ref, o2_ref, o3_ref)
    l_refs = (l1_ref, l2_ref, l3_ref)
    for g, dil in enumerate(DILATIONS):
        rows = tm // dil
        for r in range(dil):
            dst = pl.ds(r, rows, stride=dil) if dil > 1 else slice(None)
            lnat_ref[g, dst, :] = l_refs[g][r]
            for h in range(N_HEADS):
                onat_ref[g * N_HEADS + h, dst, :] = o_refs[g][r, :, h * HEAD_DIM:(h + 1) * HEAD_DIM]
    l1, l2, l3 = lnat_ref[0], lnat_ref[1], lnat_ref[2]
    mx = jnp.maximum(jnp.maximum(l1, l2), l3)
    e1, e2, e3 = jnp.exp(l1 - mx), jnp.exp(l2 - mx), jnp.exp(l3 - mx)
    tot = (e1 + e2) + e3
    w1, w2, w3 = e1 / tot, e2 / tot, e3 / tot
    for h in range(N_HEADS):
        _ride(rider, h, N_HEADS)
        a = onat_ref[h] * w1[:, h:h + 1]
        a = a + onat_ref[N_HEADS + h] * w2[:, h:h + 1]
        a = a + onat_ref[2 * N_HEADS + h] * w3[:, h:h + 1]
        a_ref[:, h * HEAD_DIM:(h + 1) * HEAD_DIM] = a.astype(BF16)
    out_ref[...] = x_ref[...] + jnp.dot(a_ref[...], w_ref[...], preferred_element_type=F32)


def _merge_wo(x, os_, ls_, w_o, *, tm, shift=None):
    m, d = x.shape
    rowblk = lambda i: (i, 0)
    deint = lambda i: (0, i, 0)
    in_specs = [pl.BlockSpec((tm, d), rowblk)]
    in_specs += [pl.BlockSpec((dil, tm // dil, GW), deint) for dil in DILATIONS]
    in_specs += [pl.BlockSpec((dil, tm // dil, LANES), deint) for dil in DILATIONS]
    in_specs += [pl.BlockSpec((GW, d), lambda i: (0, 0), pipeline_mode=pl.Buffered(1))]
    outs = _call(
        functools.partial(_merge_wo_kernel, tm=tm),
        name="merge_wo", grid=(m // tm,), in_specs=in_specs, args=[x, *os_, *ls_, w_o],
        out_specs=[pl.BlockSpec((tm, d), rowblk)],
        out_shape=[jax.ShapeDtypeStruct((m, d), F32)],
        scratch=[pltpu.VMEM((N_GROUPS * N_HEADS, tm, LANES), F32),
                 pltpu.VMEM((N_GROUPS, tm, LANES), F32),
                 pltpu.VMEM((tm, GW), BF16)],
        sem="arbitrary", shift=shift)
    return outs[0], (outs[-1] if shift is not None else None)


def _bf16_round(v):
    return v.astype(BF16).astype(F32)


def _attn_sample_kernel(q_ref, kvn_ref, c1_ref, c2_ref, c3_ref, o_ref):
    outs, lses = [], []
    for g, c_ref in enumerate((c1_ref, c2_ref, c3_ref)):
        q = _bf16_round(q_ref[g, 0])
        kn = _bf16_round(kvn_ref[g, 0, 0])
        vn = _bf16_round(kvn_ref[g, 1, 0])
        kc = _bf16_round(c_ref[0, :, 0, 0])
        vc = _bf16_round(c_ref[0, :, 0, 1])
        s_c = jnp.sum(kc * q[None], axis=-1, keepdims=True)
        s_n = jnp.sum(kn * q, axis=-1, keepdims=True)
        m = jnp.maximum(jnp.max(s_c, axis=0), s_n)
        p_c = jnp.exp(s_c - m[None])
        p_n = jnp.exp(s_n - m)
        den = jnp.sum(p_c, axis=0) + p_n
        o = jnp.sum(_bf16_round(p_c / den[None]) * vc, axis=0) + _bf16_round(p_n / den) * vn
        outs.append(o)
        lses.append(m + jnp.log(den))
    mx = jnp.maximum(jnp.maximum(lses[0], lses[1]), lses[2])
    es = [jnp.exp(l - mx) for l in lses]
    tot = (es[0] + es[1]) + es[2]
    acc = outs[0] * (es[0] / tot)
    acc = acc + outs[1] * (es[1] / tot)
    acc = acc + outs[2] * (es[2] / tot)
    o_ref[0] = acc


def _attn_sample(q, kvn, caches):
    b = q.shape[1]
    nk = N_KEYS - 1
    views = [c.reshape(b, nk, DILATIONS[g], 2, N_HEADS, HEAD_DIM) for g, c in enumerate(caches)]
    cspec = pl.BlockSpec((1, nk, 1, 2, N_HEADS, HEAD_DIM), lambda i: (i, 0, 0, 0, 0, 0))
    return pl.pallas_call(
        _attn_sample_kernel,
        grid=(b,),
        in_specs=[
            pl.BlockSpec((N_GROUPS, 1, N_HEADS, HEAD_DIM), lambda i: (0, i, 0, 0)),
            pl.BlockSpec((N_GROUPS, 2, 1, N_HEADS, HEAD_DIM), lambda i: (0, 0, i, 0, 0)),
            cspec, cspec, cspec,
        ],
        out_specs=pl.BlockSpec((1, N_HEADS, HEAD_DIM), lambda i: (i, 0, 0)),
        out_shape=jax.ShapeDtypeStruct((b, N_HEADS, HEAD_DIM), F32),
        compiler_params=_params("parallel"),
        name="attn_sample",
    )(q, kvn, *views)


def _matmul_res_kernel(x_ref, a_ref, w_ref, o_ref):
    o_ref[...] = x_ref[...] + jnp.dot(a_ref[...].astype(BF16), w_ref[...], preferred_element_type=F32)


def _matmul_res(x, a, w):
    vm = pl.BlockSpec(memory_space=pltpu.VMEM)
    return pl.pallas_call(
        _matmul_res_kernel,
        in_specs=[vm] * 3,
        out_specs=vm,
        out_shape=jax.ShapeDtypeStruct(x.shape, F32),
        compiler_params=pltpu.CompilerParams(vmem_limit_bytes=VMEM_LIMIT_BYTES),
        name="matmul_res",
    )(x, a, w)


def kernel(x_prompt, x_sample, state_h, state_conv, cache_kv_g1, cache_kv_g2, cache_kv_g3,
           ffn1_norm, ffn1_w_up, ffn1_w_down, mix_norm, ffn2_norm, ffn2_w_up, ffn2_w_down,
           rg_w_in, rg_conv_w, rg_conv_b, rg_gate_a_w, rg_gate_a_b, rg_gate_x_w, rg_gate_x_b,
           rg_lambda, rg_w_out, kv_norm, w_kv, attn_w_q, attn_w_o, final_norm):
    caches = [cache_kv_g1, cache_kv_g2, cache_kv_g3]
    bp, sp, d = x_prompt.shape
    db, ds, _ = x_sample.shape
    assert bp == 1 and ds == 1 and d == D_MODEL
    assert all(c.shape[1] == w for c, w in zip(caches, WINDOWS))
    past = PAST_LEN

    xp = x_prompt.reshape(sp, d)
    xs = x_sample.reshape(db, d)
    row = lambda v: v.reshape(1, -1)
    bf = lambda w: w.astype(BF16)

    half = ROPE_DIM // 2
    inv = ROPE_THETA ** (-jnp.arange(half, dtype=F32) / half)
    inv_lane = jnp.concatenate([inv, inv, jnp.zeros((LANES - ROPE_DIM,), F32)]).reshape(1, LANES)
    rope_p = _rope_tables(inv_lane, sp, pos0=0, pos_stride=1)
    rope_s = _rope_tables(inv_lane, db, pos0=past, pos_stride=0)
    fg = row(final_norm)

    def ffn_pair(xp, xs, g, w_up, w_down, layer, final=False, next_gain=None, shift=None):
        eg = fg if next_gain is None else row(next_gain)
        kw = dict(layer=layer, final=final, norm_next=next_gain is not None, tf=512)
        xs, w_bf, xns, _ = _ffn(xs, row(g[layer]), (w_up, w_down), eg, tm=db, **kw)
        xp, _, xnp, shifted = _ffn(xp, row(g[layer]), w_bf, eg, tm=512, shift=shift, **kw)
        return xp, xs, xnp, xns, shifted

    t12 = 128
    whole = lambda g: _ShiftJob(caches[g], None, 0, db * (WINDOWS[g] // t12), t12)
    t3 = 256
    nblk3 = db * (WINDOWS[2] // t3)
    shifted3, next_blk = None, 0

    def ride(steps):
        nonlocal next_blk
        count = min(steps, nblk3 - next_blk)
        job = _ShiftJob(caches[2], shifted3, next_blk, count, t3)
        next_blk += count
        return job

    xp, xs, _, _, shifted2 = ffn_pair(xp, xs, ffn1_norm, ffn1_w_up, ffn1_w_down, 0,
                                      shift=whole(1))
    rg = (row(mix_norm[0]), bf(rg_w_in[0]), rg_conv_w[0], row(rg_conv_b[0]), bf(rg_gate_a_w[0]),
          row(rg_gate_a_b[0]), bf(rg_gate_x_w[0]), row(rg_gate_x_b[0]), row(rg_lambda[0]),
          bf(rg_w_out[0]))
    xp, p_h, p_conv = _rg_prompt(xp, *rg, tm=256)
    xs, s_h, s_conv = _rg_sample(xs, *rg, state_h[0], state_conv[0].reshape(db, -1))
    xp, xs, xnp, xns, shifted1 = ffn_pair(xp, xs, ffn2_norm, ffn2_w_up, ffn2_w_down, 0,
                                          next_gain=kv_norm, shift=whole(0))

    wkv = bf(w_kv)
    tm_p = 256
    kv_p, p_kv, kv_s = [], [], []
    for g in range(N_GROUPS):
        common = dict(rope=(True, False), scale=(1.0, 1.0))
        kv, tail, shifted3 = _proj(xnp, rope_p, wkv, (g, N_GROUPS + g), dil=DILATIONS[g],
                                   tm=tm_p, out_dtype=BF16, tail=WINDOWS[g],
                                   shift=ride(sp // tm_p), **common)
        kv_p.append(kv)
        p_kv.append(jnp.swapaxes(tail, 0, 1).reshape(1, WINDOWS[g], 2, N_HEADS, HEAD_DIM))
        kv_s.append(_proj(xns, rope_s, wkv, (g, N_GROUPS + g), dil=1, tm=db, out_dtype=F32,
                          **common)[0])
    kvn = jnp.stack(kv_s).reshape(N_GROUPS, 2, db, N_HEADS, HEAD_DIM)

    xp, xs, xnp, xns, _ = ffn_pair(xp, xs, ffn1_norm, ffn1_w_up, ffn1_w_down, 1,
                                   next_gain=mix_norm[1])
    wq = bf(attn_w_q[0])
    wo = bf(attn_w_o[0])
    qcommon = dict(rope=(True,), scale=(HEAD_DIM ** -0.5,))
    tq = 512
    og, lg, q_s = [], [], []
    for g in range(N_GROUPS):
        q, _, shifted3 = _proj(xnp, rope_p, wq, (g,), dil=DILATIONS[g], tm=tm_p, out_dtype=BF16, shift=ride(sp // tm_p), **qcommon)
        o, lse, shifted3 = _attn_group(q, kv_p[g], tq=tq, shift=ride(sp // tq))
        og.append(o)
        lg.append(lse)
        q_s.append(_proj(xns, rope_s, wq, (g,), dil=1, tm=db,
                         out_dtype=F32, **qcommon)[0])
    xp, shifted3 = _merge_wo(xp, og, lg, wo, tm=tm_p, shift=ride(sp // tm_p))
    assert next_blk == nblk3
    qs = jnp.stack(q_s).reshape(N_GROUPS, db, N_HEADS, HEAD_DIM)
    xs = _matmul_res(xs, _attn_sample(qs, kvn, caches).reshape(db, GW), wo)
    xp, xs, _, _, _ = ffn_pair(xp, xs, ffn2_norm, ffn2_w_up, ffn2_w_down, 1, final=True)

    s_kv = _insert_rows([kvn[g] for g in range(N_GROUPS)], [shifted1, shifted2, shifted3])

    return (xp.reshape(bp, sp, d), xs.reshape(db, ds, d),
            p_h.reshape(1, bp, D_RNN), p_conv.reshape(1, bp, CONV_WIDTH - 1, D_RNN),
            p_kv[0], p_kv[1], p_kv[2],
            s_h.reshape(1, db, D_RNN), s_conv.reshape(1, db, CONV_WIDTH - 1, D_RNN),
            s_kv[0], s_kv[1], s_kv[2])
```

```python
import functools
from typing import NamedTuple, Optional

import jax
import jax.numpy as jnp
from jax import lax
from jax.experimental import pallas as pl
from jax.experimental.pallas import tpu as pltpu

F32 = jnp.float32
BF16 = jnp.bfloat16

D_MODEL = 2048
D_FF = 5632
D_RNN = 2048
N_RG_BLOCKS = 16
RG_BLOCK = D_RNN // N_RG_BLOCKS
CONV_WIDTH = 4
RG_C = 8.0
HEAD_DIM = 128
N_HEADS = 16
N_GROUPS = 3
WINDOWS = (128, 512, 2048)
DILATIONS = (1, 4, 16)
N_KEYS = 129
ROPE_DIM = 32
ROPE_THETA = 500000.0
EPS = 1e-6
PAST_LEN = 8192
GW =N_HEADS * HEAD_DIM

LANES = 128
SUBLANES = 8
VMEM_LIMIT_BYTES = 60 * 1024 * 1024


def _params(*sem):
    return pltpu.CompilerParams(dimension_semantics=sem, vmem_limit_bytes=VMEM_LIMIT_BYTES)


def _rms(x, g):
    var = jnp.mean(x * x, axis=-1, keepdims=True)
    return (x * lax.rsqrt(var + EPS)) * g


def _softplus(z):
    return jnp.maximum(z, 0.0) + jnp.log1p(jnp.exp(-jnp.abs(z)))


def _sigmoid(z):
    return 0.5 * jnp.tanh(0.5 * z) + 0.5


class _ShiftJob(NamedTuple):
    cache: jax.Array
    prev: Optional[jax.Array]
    first: int
    count: int
    t: int


class _Rider:
    def __init__(self, c_ref, nxt_ref, o_ref):
        self.c, self.nxt, self.o = c_ref, nxt_ref, o_ref
        self.t = c_ref.shape[1]

    def part(self, p, n):
        rp = self.t // n
        lo, hi = p * rp, (p + 1) * rp
        if p == n - 1:
            self.o[0, lo:hi - 1] = self.c[0, lo + 1:hi]
            self.o[0, hi - 1] = self.nxt[0, 0]
        else:
            self.o[0, lo:hi] = self.c[0, lo + 1:hi + 1]

    def loop_part(self, it, n):
        rp = self.t // n
        lo = jnp.minimum(it, n - 2) * rp
        self.o[0, pl.ds(lo, rp)] = self.c[0, pl.ds(lo + 1, rp)]


def _ride(rider, p, n):
    if rider is not None:
        rider.part(p, n)


def _with_shift(kernel_fn, n_in, n_out, n_shift_in):
    def wrapped(*refs):
        outs_at = n_in + n_shift_in
        rider = _Rider(refs[n_in], refs[n_in + 1], refs[outs_at + n_out])
        kernel_fn(*refs[:n_in], *refs[outs_at:outs_at + n_out], *refs[outs_at + n_out + 1:],
                  rider=rider)
    return wrapped


def _call(kernel_fn, *, name, grid, in_specs, args, out_specs, out_shape, scratch, sem, shift=None):
    aliases = {}
    if shift is not None:
        b, w = shift.cache.shape[:2]
        nblk = w // shift.t
        tail = shift.cache.shape[2:]
        zeros = (0,) * len(tail)

        def blk(*idx):
            step = idx[0]
            for extent, i in zip(grid[1:], idx[1:]):
                step = step * extent + i
            k = shift.first + jnp.minimum(step, shift.count - 1)
            return k // nblk, k % nblk

        def nxt(*idx):
            bi, ji = blk(*idx)
            return bi, jnp.minimum((ji + 1) * shift.t, w - 1)

        shift_in = [pl.BlockSpec((1, shift.t) + tail, lambda *idx: blk(*idx) + zeros),
                    pl.BlockSpec((1, 1) + tail, lambda *idx: nxt(*idx) + zeros)]
        shift_args = [shift.cache, shift.cache]
        if shift.prev is not None:
            shift_in.append(pl.BlockSpec(memory_space=pl.ANY))
            shift_args.append(shift.prev)
            aliases = {len(in_specs) + 2: len(out_specs)}
        kernel_fn = _with_shift(kernel_fn, len(in_specs), len(out_specs), len(shift_in))
        in_specs = list(in_specs) + shift_in
        args = list(args) + shift_args
        out_specs = list(out_specs) + [
            pl.BlockSpec((1, shift.t) + tail, lambda *idx: blk(*idx) + zeros)]
        out_shape = list(out_shape) + [jax.ShapeDtypeStruct(shift.cache.shape, shift.cache.dtype)]
    sems = (sem,) if isinstance(sem, str) else sem
    return pl.pallas_call(
        kernel_fn, grid=grid, in_specs=in_specs, out_specs=out_specs, out_shape=out_shape,
        input_output_aliases=aliases, scratch_shapes=scratch, compiler_params=_params(*sems),
        name=name,
    )(*args)


def _insert_rows_kernel(*refs):
    n = len(refs) // 3
    for new_ref, o_ref in zip(refs[:n], refs[2 * n:]):
        o_ref[0, 0] = new_ref[:, 0]


def _insert_rows(kv_new, shifted):
    n = len(shifted)
    b = shifted[0].shape[0]
    tail = shifted[0].shape[2:]
    row_spec = lambda w: pl.BlockSpec((1, 1) + tail, lambda i: (i, w - 1, 0, 0, 0))
    return pl.pallas_call(
        _insert_rows_kernel,
        grid=(b,),
        in_specs=[pl.BlockSpec((2, 1) + tail[1:], lambda i: (0, i, 0, 0))] * n
        + [pl.BlockSpec(memory_space=pl.ANY)] * n,
        out_specs=[row_spec(s.shape[1]) for s in shifted],
        out_shape=[jax.ShapeDtypeStruct(s.shape, s.dtype) for s in shifted],
        input_output_aliases={n + k: k for k in range(n)},
        compiler_params=_params("arbitrary"),
        name="insert_rows",
    )(*kv_new, *shifted)


def _ffn_kernel(x_ref, g_ref, wg_ref, wu_ref, wd_ref, fg_ref, o_ref, *rest, nf, final, emit,
                norm_next, rider=None):
    xn_ref = rest[-1]
    j = pl.program_id(1)

    @pl.when(j == 0)
    def _():
        xn_ref[...] = _rms(x_ref[...], g_ref[...]).astype(BF16)
        o_ref[...] = jnp.zeros_like(o_ref)

    wg, wu, wd = wg_ref[...], wu_ref[...], wd_ref[...]
    if emit:
        wg, wu, wd = wg.astype(BF16), wu.astype(BF16), wd.astype(BF16)
        rest[0][...], rest[1][...], rest[2][...] = wg, wu, wd

    xn = xn_ref[...]
    _ride(rider, 0, 4)
    gate = jnp.dot(xn, wg, preferred_element_type=F32)
    _ride(rider, 1, 4)
    up = jnp.dot(xn, wu, preferred_element_type=F32)
    _ride(rider, 2, 4)
    h = (gate * jax.nn.sigmoid(gate)) * up
    _ride(rider, 3, 4)
    o_ref[...] += jnp.dot(h.astype(BF16), wd, preferred_element_type=F32)

    @pl.when(j == nf - 1)
    def _():
        y = x_ref[...] + 0.5 * o_ref[...]
        if final:
            y = _rms(y, fg_ref[...])
        o_ref[...] = y
        if norm_next:
            rest[-2][...] = _rms(y, fg_ref[...]).astype(BF16)


def _ffn(x, g, weights, fg, *, layer, final, tm, tf, norm_next=False, shift=None):
    m, d = x.shape
    assert not (final and norm_next)
    emit = len(weights) == 2
    nf = D_FF // tf
    if emit:
        w_up, w_down = weights
        w_args = [w_up, w_up, w_down]
        w_specs = [
            pl.BlockSpec((None, d, tf), lambda i, j: (layer, 0, j)),
            pl.BlockSpec((None, d, tf), lambda i, j: (layer, 0, j + nf)),
            pl.BlockSpec((None, tf, d), lambda i, j: (layer, j, 0)),
        ]
    else:
        w_args = list(weights)
        w_specs = [None] * 3
    cast_specs = [
        pl.BlockSpec((d, tf), lambda i, j: (0, j)),
        pl.BlockSpec((d, tf), lambda i, j: (0, j)),
        pl.BlockSpec((tf, d), lambda i, j: (j, 0)),
    ]
    cast_shapes = [jax.ShapeDtypeStruct((d, D_FF), BF16)] * 2 + [jax.ShapeDtypeStruct((D_FF, d), BF16)]
    row_spec = pl.BlockSpec((tm, d), lambda i, j: (i, 0))
    extra_specs = (cast_specs if emit else []) + ([row_spec] if norm_next else [])
    extra_shapes = (cast_shapes if emit else []) + (
        [jax.ShapeDtypeStruct((m, d), BF16)] if norm_next else [])
    outs = _call(
        functools.partial(_ffn_kernel, nf=nf, final=final, emit=emit, norm_next=norm_next),
        name="ffn", grid=(m // tm, nf),
        in_specs=[
            pl.BlockSpec((tm, d), lambda i, j: (i, 0)),
            pl.BlockSpec((1, d), lambda i, j: (0, 0)),
            *(w_specs if emit else cast_specs),
            pl.BlockSpec((1, d), lambda i, j: (0, 0)),
        ],
        args=[x, g, *w_args, fg],
        out_specs=[row_spec] + extra_specs,
        out_shape=[jax.ShapeDtypeStruct((m, d), F32)] + extra_shapes,
        scratch=[pltpu.VMEM((tm, d), BF16)], sem=("arbitrary", "arbitrary"), shift=shift)
    n_cast = 3 if emit else 0
    return (outs[0], tuple(outs[1:4]) if emit else None,
            outs[1 + n_cast] if norm_next else None, outs[-1] if shift is not None else None)


def _rg_gates(xc, gaw, gab, gxw, gxb, sp):
    xcb = xc.astype(BF16)
    r = _sigmoid(jnp.dot(xcb, gaw, preferred_element_type=F32) + gab)
    ig = _sigmoid(jnp.dot(xcb, gxw, preferred_element_type=F32) + gxb)
    log_a = (-RG_C * r) * sp
    a = jnp.exp(log_a)
    b = jnp.sqrt(-jnp.tanh(log_a) * (1.0 + a * a)) * (ig * xc)
    return a, b


def _rg_prompt_kernel(x_ref, g_ref, win_ref, cw_ref, cb_ref, gaw_ref, gab_ref, gxw_ref, gxb_ref,
                      lam_ref, wout_ref, o_ref, hlast_ref, conv_ref,
                      ext_ref, gate_ref, y_ref, h_ref, *, tm, rider=None):
    i = pl.program_id(0)
    pad = SUBLANES

    @pl.when(i == 0)
    def _():
        ext_ref[0:pad, :] = jnp.zeros((pad, D_RNN), F32)
        h_ref[...] = jnp.zeros_like(h_ref)

    x = x_ref[...]
    y_ref[...] = _rms(x, g_ref[...]).astype(BF16)
    gate_ref[...] = jnp.dot(y_ref[...], win_ref[:, 0:D_RNN], preferred_element_type=F32)
    ext_ref[pad:pad + tm, :] = jnp.dot(y_ref[...], win_ref[:, D_RNN:2 * D_RNN],
                                       preferred_element_type=F32)

    sub = lax.broadcasted_iota(jnp.int32, (tm // SUBLANES, SUBLANES, RG_BLOCK), 1)
    for n in range(N_RG_BLOCKS):
        cs = slice(n * RG_BLOCK, (n + 1) * RG_BLOCK)
        _ride(rider, n, N_RG_BLOCKS)
        e = ext_ref[:, cs]
        taps = [pltpu.roll(e, CONV_WIDTH - 1 - j, 0)[pad:pad + tm, :]
                for j in range(CONV_WIDTH - 1)] + [e[pad:pad + tm, :]]
        xc = cb_ref[:, cs] + taps[0] * cw_ref[0:1, cs]
        for j in range(1, CONV_WIDTH):
            xc = xc + taps[j] * cw_ref[j:j + 1, cs]
        sp = _softplus(-lam_ref[:, cs])
        a, b = _rg_gates(xc, gaw_ref[n], gab_ref[:, cs], gxw_ref[n], gxb_ref[:, cs], sp)
        a = a.reshape(tm // SUBLANES, SUBLANES, RG_BLOCK)
        b = b.reshape(tm // SUBLANES, SUBLANES, RG_BLOCK)
        s = 1
        while s < SUBLANES:
            keep = sub >= s
            b = b + a * jnp.where(keep, pltpu.roll(b, s, 1), 0.0)
            a = a * jnp.where(keep, pltpu.roll(a, s, 1), 1.0)
            s *= 2
        carry = jnp.broadcast_to(h_ref[:, cs], (SUBLANES, RG_BLOCK))
        hs = []
        for k in range(tm // SUBLANES):
            hk = a[k] * carry + b[k]
            hs.append(hk)
            carry = jnp.broadcast_to(hk[SUBLANES - 1:SUBLANES, :], (SUBLANES, RG_BLOCK))
        h = jnp.concatenate(hs, axis=0)
        h_ref[:, cs] = carry[0:1, :]
        y_ref[:, cs] = (h * jax.nn.gelu(gate_ref[:, cs])).astype(BF16)

    o_ref[...] = x + jnp.dot(y_ref[...], wout_ref[...], preferred_element_type=F32)
    hlast_ref[...] = h_ref[...]
    conv_ref[...] = ext_ref[pad + tm - 3:pad + tm, :]
    ext_ref[0:pad, :] = ext_ref[tm:tm + pad, :]


def _rg_prompt(x, g, w_in, cw, cb, gaw, gab, gxw, gxb, lam, w_out, *, tm, shift=None):
    m, d = x.shape
    const2 = lambda i: (0, 0)
    const3 = lambda i: (0, 0, 0)
    one = pl.Buffered(1)
    return _call(
        functools.partial(_rg_prompt_kernel, tm=tm),
        name="rg_prompt", grid=(m // tm,),
        in_specs=[
            pl.BlockSpec((tm, d), lambda i: (i, 0)),
            pl.BlockSpec((1, d), const2),
            pl.BlockSpec((d, 2 * D_RNN), const2, pipeline_mode=one),
            pl.BlockSpec((CONV_WIDTH, D_RNN), const2),
            pl.BlockSpec((1, D_RNN), const2),
            pl.BlockSpec((N_RG_BLOCKS, RG_BLOCK, RG_BLOCK), const3),
            pl.BlockSpec((1, D_RNN), const2),
            pl.BlockSpec((N_RG_BLOCKS, RG_BLOCK, RG_BLOCK), const3),
            pl.BlockSpec((1, D_RNN), const2),
            pl.BlockSpec((1, D_RNN), const2),
            pl.BlockSpec((D_RNN, d), const2, pipeline_mode=one),
        ],
        args=[x, g, w_in, cw, cb, gaw, gab, gxw, gxb, lam, w_out],
        out_specs=[
            pl.BlockSpec((tm, d), lambda i: (i, 0)),
            pl.BlockSpec((1, D_RNN), const2),
            pl.BlockSpec((CONV_WIDTH - 1, D_RNN), const2),
        ],
        out_shape=[
            jax.ShapeDtypeStruct((m, d), F32),
            jax.ShapeDtypeStruct((1, D_RNN), F32),
            jax.ShapeDtypeStruct((CONV_WIDTH - 1, D_RNN), F32),
        ],
        scratch=[
            pltpu.VMEM((tm + SUBLANES, D_RNN), F32),
            pltpu.VMEM((tm, D_RNN), F32),
            pltpu.VMEM((tm, D_RNN), BF16),
            pltpu.VMEM((1, D_RNN), F32),
        ],
        sem="arbitrary", shift=shift)


def _rg_sample_kernel(x_ref, g_ref, win_ref, cw_ref, cb_ref, gaw_ref, gab_ref, gxw_ref, gxb_ref,
                      lam_ref, wout_ref, h0_ref, c0_ref, o_ref, hnew_ref, cnew_ref, y_ref):
    x = x_ref[...]
    xn = _rms(x, g_ref[...]).astype(BF16)
    gate = jnp.dot(xn, win_ref[:, 0:D_RNN], preferred_element_type=F32)
    rec = jnp.dot(xn, win_ref[:, D_RNN:2 * D_RNN], preferred_element_type=F32)
    for n in range(N_RG_BLOCKS):
        cs = slice(n * RG_BLOCK, (n + 1) * RG_BLOCK)
        taps = [c0_ref[:, j * D_RNN + n * RG_BLOCK:j * D_RNN + (n + 1) * RG_BLOCK]
                for j in range(CONV_WIDTH - 1)] + [rec[:, cs]]
        xc = cb_ref[:, cs] + taps[0] * cw_ref[0:1, cs]
        for j in range(1, CONV_WIDTH):
            xc = xc + taps[j] * cw_ref[j:j + 1, cs]
        sp = _softplus(-lam_ref[:, cs])
        a, b = _rg_gates(xc, gaw_ref[n], gab_ref[:, cs], gxw_ref[n], gxb_ref[:, cs], sp)
        h = a * h0_ref[:, cs] + b
        hnew_ref[:, cs] = h
        y_ref[:, cs] = (h * jax.nn.gelu(gate[:, cs])).astype(BF16)
    o_ref[...] = x + jnp.dot(y_ref[...], wout_ref[...], preferred_element_type=F32)
    cnew_ref[:, 0:2 * D_RNN] = c0_ref[:, D_RNN:3 * D_RNN]
    cnew_ref[:, 2 * D_RNN:3 * D_RNN] = rec


def _rg_sample(x, g, w_in, cw, cb, gaw, gab, gxw, gxb, lam, w_out, h0, c0):
    m, d = x.shape
    vm = pl.BlockSpec(memory_space=pltpu.VMEM)
    return pl.pallas_call(
        _rg_sample_kernel,
        in_specs=[vm] * 13,
        out_specs=[vm] * 3,
        out_shape=[
            jax.ShapeDtypeStruct((m, d), F32),
            jax.ShapeDtypeStruct((m, D_RNN), F32),
            jax.ShapeDtypeStruct((m, (CONV_WIDTH - 1) * D_RNN), F32),
        ],
        scratch_shapes=[pltpu.VMEM((m, D_RNN), BF16)],
        compiler_params=pltpu.CompilerParams(vmem_limit_bytes=VMEM_LIMIT_BYTES),
        name="rg_sample",
    )(x, g, w_in, cw, cb, gaw, gab, gxw, gxb, lam, w_out, h0, c0)


_PROJ_CHUNK = 512


def _rope_kernel(inv_ref, cos_ref, shi_ref, slo_ref, *, tr, pos0, pos_stride):
    half = ROPE_DIM // 2
    r = lax.broadcasted_iota(jnp.int32, (tr, LANES), 0)
    pos = pos0 + (pl.program_id(0) * tr + r) * pos_stride
    ang = pos.astype(F32) * inv_ref[...]
    lane = lax.broadcasted_iota(jnp.int32, (tr, LANES), 1)
    sin = jnp.sin(ang)
    cos_ref[...] = jnp.cos(ang)
    shi_ref[...] = jnp.where((lane >= half) & (lane < ROPE_DIM), sin, 0.0)
    slo_ref[...] = jnp.where(lane < half, -sin, 0.0)


def _rope_tables(inv_lane, n_rows, *, pos0, pos_stride):
    tr = min(n_rows, 1024)
    spec = pl.BlockSpec((tr, LANES), lambda i: (i, 0))
    return pl.pallas_call(
        functools.partial(_rope_kernel, tr=tr, pos0=pos0, pos_stride=pos_stride),
        grid=(n_rows // tr,),
        in_specs=[pl.BlockSpec((1, LANES), lambda i: (0, 0))],
        out_specs=[spec] * 3,
        out_shape=[jax.ShapeDtypeStruct((n_rows, LANES), F32)] * 3,
        compiler_params=_params("parallel"),
        name="rope_tables",
    )(inv_lane)


def _proj_kernel(*refs, tm, dil, n_mats, rope, scale, out_dtype, has_tail, rider=None):
    xn_ref, cos_ref, shi_ref, slo_ref = refs[:4]
    w_refs = refs[4:4 + n_mats]
    o_ref = refs[4 + n_mats]
    tail_ref = refs[5 + n_mats] if has_tail else None
    slab_ref = refs[-1]
    half = ROPE_DIM // 2
    th = tm

    for t0 in range(0, tm, th):
        rows = slice(t0, t0 + th)
        urows = slice(t0 // dil, (t0 + th) // dil)
        cos, s_hi, s_lo = cos_ref[rows, :], shi_ref[rows, :], slo_ref[rows, :]

        n_chunks = GW // _PROJ_CHUNK
        for k in range(n_mats):
            for c0 in range(0, GW, _PROJ_CHUNK):
                _ride(rider, k * n_chunks + c0 // _PROJ_CHUNK, n_mats * n_chunks)
                acc = jnp.dot(xn_ref[rows, :], w_refs[k][:, c0:c0 + _PROJ_CHUNK],
                              preferred_element_type=F32)
                for c in range(c0, c0 + _PROJ_CHUNK, LANES):
                    a = acc[:, c - c0:c - c0 + LANES]
                    if rope[k]:
                        a = (a * cos + pltpu.roll(a, half, 1) * s_hi
                             + pltpu.roll(a, LANES - half, 1) * s_lo)
                    if scale[k] != 1.0:
                        a = a * scale[k]
                    if has_tail:
                        tail_ref[k, rows, c:c + LANES] = a
                    if dil == 1:
                        o_ref[k, 0, rows, c:c + LANES] = a.astype(out_dtype)
                    else:
                        s = c // LANES % slab_ref.shape[0]
                        slab_ref[s, 0:th, :] = a
                        for r_ in range(dil):
                            o_ref[k, r_, urows, c:c + LANES] = (
                                slab_ref[s, pl.ds(r_, th // dil, stride=dil), :].astype(out_dtype))


def _proj(xn, tables, w, col_blocks, *, dil, rope, scale, tm, out_dtype, tail=0, shift=None):
    m, d = xn.shape
    n_mats = len(col_blocks)
    assert m % tm == 0 and tm % dil == 0
    lrows = m // dil
    tail_rows = max(tail, tm) if tail else 0
    i0 = (m - tail_rows) // tm if tail else 0
    in_specs = [pl.BlockSpec((tm, d), lambda i: (i, 0))] + [
        pl.BlockSpec((tm, LANES), lambda i: (i, 0))] * 3
    for cb in col_blocks:
        in_specs.append(pl.BlockSpec((d, GW), lambda i, cb=cb: (0, cb), pipeline_mode=pl.Buffered(1)))
    out_specs = [pl.BlockSpec((n_mats, dil, tm // dil, GW), lambda i: (0, 0, i, 0))]
    out_shape = [jax.ShapeDtypeStruct((n_mats, dil, lrows, GW), out_dtype)]
    if tail:
        out_specs.append(pl.BlockSpec((n_mats, tm, GW), lambda i: (0, jnp.maximum(i - i0, 0), 0)))
        out_shape.append(jax.ShapeDtypeStruct((n_mats, tail_rows, GW), F32))
    outs = _call(
        functools.partial(_proj_kernel, tm=tm, dil=dil, n_mats=n_mats, rope=tuple(rope),
                          scale=tuple(scale), out_dtype=out_dtype, has_tail=bool(tail)),
        name="proj", grid=(m // tm,), in_specs=in_specs, args=[xn, *tables] + [w] * n_mats,
        out_specs=out_specs, out_shape=out_shape, scratch=[pltpu.VMEM((2, tm, LANES), F32)],
        sem="arbitrary", shift=shift)
    return (outs[0], outs[1][:, tail_rows - tail:] if tail else None,
            outs[-1] if shift is not None else None)


_ATTN_HEADS_PER_ITER = 2


def _attn_kernel(q_ref, kc_ref, kp_ref, vc_ref, vp_ref, o_ref, lse_ref, *, tq, nb, rider=None):
    ub = pl.program_id(0) % nb
    blk = LANES
    nsub = tq // blk
    row = lax.broadcasted_iota(jnp.int32, (blk, blk), 0)
    col = lax.broadcasted_iota(jnp.int32, (blk, blk), 1)
    lane = lax.broadcasted_iota(jnp.int32, (blk, LANES), 1)
    cur_ok = col <= row
    prev_tri = col >= row
    neg = -jnp.inf
    dn = (((1,), (1,)), ((), ()))

    first_ok = jnp.logical_and(prev_tri, ub > 0)
    lse_ref[...] = jnp.zeros_like(lse_ref)

    rows = [slice(c * blk, (c + 1) * blk) for c in range(nsub)]
    units = [(e, c) for e in range(_ATTN_HEADS_PER_ITER) for c in range(nsub)]

    n_iter = N_HEADS // _ATTN_HEADS_PER_ITER

    def heads(it, carry):
        if rider is not None:
            rider.loop_part(it, n_iter)
        hn = [it * _ATTN_HEADS_PER_ITER + e for e in range(_ATTN_HEADS_PER_ITER)]
        hs = [pl.ds(pl.multiple_of(h * HEAD_DIM, HEAD_DIM), HEAD_DIM) for h in hn]
        ks = [[kp_ref[:, s]] + [kc_ref[rs, s] for rs in rows] for s in hs]
        vs = [[vp_ref[:, s]] + [vc_ref[rs, s] for rs in rows] for s in hs]
        scores = {}
        for e, c in units:
            q = q_ref[rows[c], hs[e]]
            prev_ok = prev_tri if c > 0 else first_ok
            s_p = jnp.where(prev_ok, lax.dot_general(q, ks[e][c], dn, preferred_element_type=F32), neg)
            s_c = jnp.where(cur_ok, lax.dot_general(q, ks[e][c + 1], dn, preferred_element_type=F32), neg)
            scores[e, c] = (s_p, s_c)
        probs = {}
        lse_new = [lse_ref[rs, :] for rs in rows]
        for e, c in units:
            s_p, s_c = scores[e, c]
            m = jnp.max(jnp.maximum(s_p, s_c), axis=1, keepdims=True)
            p_p = jnp.exp(s_p - m)
            p_c = jnp.exp(s_c - m)
            den = jnp.sum(p_p + p_c, axis=1, keepdims=True)
            lse_new[c] = jnp.where(lane == hn[e], m + jnp.log(den), lse_new[c])
            probs[e, c] = (p_p.astype(BF16), p_c.astype(BF16), 1.0 / den)
        for c, rs in enumerate(rows):
            lse_ref[rs, :] = lse_new[c]
        for e, c in units:
            p_p, p_c, rden = probs[e, c]
            o = (jnp.dot(p_p, vs[e][c], preferred_element_type=F32)
                 + jnp.dot(p_c, vs[e][c + 1], preferred_element_type=F32))
            o_ref[rows[c], hs[e]] = o * rden
        return carry

    lax.fori_loop(0, n_iter, heads, 0)
    _ride(rider, n_iter - 1, n_iter)


def _attn_group(q, kv, *, tq, shift=None):
    _, dil, lrows, _ = q.shape
    s = dil * lrows
    nb = lrows // tq
    sub = tq // LANES
    qv = q.reshape(s, GW)
    kvv = kv.reshape(2, s, GW)
    prev_idx = lambda i: jnp.maximum(i * sub - 1, 0)
    outs = _call(
        functools.partial(_attn_kernel, tq=tq, nb=nb),
        name="attn_group", grid=(s // tq,),
        in_specs=[
            pl.BlockSpec((tq, GW), lambda i: (i, 0)),
            pl.BlockSpec((None, tq, GW), lambda i: (0, i, 0)),
            pl.BlockSpec((None, LANES, GW), lambda i: (0, prev_idx(i), 0)),
            pl.BlockSpec((None, tq, GW), lambda i: (1, i, 0)),
            pl.BlockSpec((None, LANES, GW), lambda i: (1, prev_idx(i), 0)),
        ],
        args=[qv, kvv, kvv, kvv, kvv],
        out_specs=[
            pl.BlockSpec((tq, GW), lambda i: (i, 0)),
            pl.BlockSpec((tq, LANES), lambda i: (i, 0)),
        ],
        out_shape=[
            jax.ShapeDtypeStruct((s, GW), F32),
            jax.ShapeDtypeStruct((s, LANES), F32),
        ],
        scratch=[], sem="arbitrary", shift=shift)
    return (outs[0].reshape(dil, lrows, GW), outs[1].reshape(dil, lrows, LANES),
            outs[-1] if shift is not None else None)


def _merge_wo_kernel(x_ref, o1_ref, o2_ref, o3_ref, l1_ref, l2_ref, l3_ref, w_ref, out_ref,
                     onat_ref, lnat_ref, a_ref, *, tm, rider=None):
    o_refs = (o1_ref, o2_ref, o3_ref)
    l_refs = (l1_ref, l2_ref, l3_ref)
    for g, dil in enumerate(DILATIONS):
        rows = tm // dil
        for r in range(dil):
            dst = pl.ds(r, rows, stride=dil) if dil > 1 else slice(None)
            lnat_ref[g, dst, :] = l_refs[g][r]
            for h in range(N_HEADS):
                onat_ref[g * N_HEADS + h, dst, :] = o_refs[g][r, :, h * HEAD_DIM:(h + 1) * HEAD_DIM]
    l1, l2, l3 = lnat_ref[0], lnat_ref[1], lnat_ref[2]
    mx = jnp.maximum(jnp.maximum(l1, l2), l3)
    e1, e2, e3 = jnp.exp(l1 - mx), jnp.exp(l2 - mx), jnp.exp(l3 - mx)
    tot = (e1 + e2) + e3
    w1, w2, w3 = e1 / tot, e2 / tot, e3 / tot
    for h in range(N_HEADS):
        _ride(rider, h, N_HEADS)
        a = onat_ref[h] * w1[:, h:h + 1]
        a = a + onat_ref[N_HEADS + h] * w2[:, h:h + 1]
        a = a + onat_ref[2 * N_HEADS + h] * w3[:, h:h + 1]
        a_ref[:, h * HEAD_DIM:(h + 1) * HEAD_DIM] = a.astype(BF16)
    out_ref[...] = x_ref[...] + jnp.dot(a_ref[...], w_ref[...], preferred_element_type=F32)


def _merge_wo(x, os_, ls_, w_o, *, tm, shift=None):
    m, d = x.shape
    rowblk = lambda i: (i, 0)
    deint = lambda i: (0, i, 0)
    in_specs = [pl.BlockSpec((tm, d), rowblk)]
    in_specs += [pl.BlockSpec((dil, tm // dil, GW), deint) for dil in DILATIONS]
    in_specs += [pl.BlockSpec((dil, tm // dil, LANES), deint) for dil in DILATIONS]
    in_specs += [pl.BlockSpec((GW, d), lambda i: (0, 0), pipeline_mode=pl.Buffered(1))]
    outs = _call(
        functools.partial(_merge_wo_kernel, tm=tm),
        name="merge_wo", grid=(m // tm,), in_specs=in_specs, args=[x, *os_, *ls_, w_o],
        out_specs=[pl.BlockSpec((tm, d), rowblk)],
        out_shape=[jax.ShapeDtypeStruct((m, d), F32)],
        scratch=[pltpu.VMEM((N_GROUPS * N_HEADS, tm, LANES), F32),
                 pltpu.VMEM((N_GROUPS, tm, LANES), F32),
                 pltpu.VMEM((tm, GW), BF16)],
        sem="arbitrary", shift=shift)
    return outs[0], (outs[-1] if shift is not None else None)


def _bf16_round(v):
    return v.astype(BF16).astype(F32)


def _attn_sample_kernel(q_ref, kvn_ref, c1_ref, c2_ref, c3_ref, o_ref):
    outs, lses = [], []
    for g, c_ref in enumerate((c1_ref, c2_ref, c3_ref)):
        q = _bf16_round(q_ref[g, 0])
        kn = _bf16_round(kvn_ref[g, 0, 0])
        vn = _bf16_round(kvn_ref[g, 1, 0])
        kc = _bf16_round(c_ref[0, :, 0, 0])
        vc = _bf16_round(c_ref[0, :, 0, 1])
        s_c = jnp.sum(kc * q[None], axis=-1, keepdims=True)
        s_n = jnp.sum(kn * q, axis=-1, keepdims=True)
        m = jnp.maximum(jnp.max(s_c, axis=0), s_n)
        p_c = jnp.exp(s_c - m[None])
        p_n = jnp.exp(s_n - m)
        den = jnp.sum(p_c, axis=0) + p_n
        o = jnp.sum(_bf16_round(p_c / den[None]) * vc, axis=0) + _bf16_round(p_n / den) * vn
        outs.append(o)
        lses.append(m + jnp.log(den))
    mx = jnp.maximum(jnp.maximum(lses[0], lses[1]), lses[2])
    es = [jnp.exp(l - mx) for l in lses]
    tot = (es[0] + es[1]) + es[2]
    acc = outs[0] * (es[0] / tot)
    acc = acc + outs[1] * (es[1] / tot)
    acc = acc + outs[2] * (es[2] / tot)
    o_ref[0] = acc


def _attn_sample(q, kvn, caches):
    b = q.shape[1]
    nk = N_KEYS - 1
    views = [c.reshape(b, nk, DILATIONS[g], 2, N_HEADS, HEAD_DIM) for g, c in enumerate(caches)]
    cspec = pl.BlockSpec((1, nk, 1, 2, N_HEADS, HEAD_DIM), lambda i: (i, 0, 0, 0, 0, 0))
    return pl.pallas_call(
        _attn_sample_kernel,
        grid=(b,),
        in_specs=[
            pl.BlockSpec((N_GROUPS, 1, N_HEADS, HEAD_DIM), lambda i: (0, i, 0, 0)),
            pl.BlockSpec((N_GROUPS, 2, 1, N_HEADS, HEAD_DIM), lambda i: (0, 0, i, 0, 0)),
            cspec, cspec, cspec,
        ],
        out_specs=pl.BlockSpec((1, N_HEADS, HEAD_DIM), lambda i: (i, 0, 0)),
        out_shape=jax.ShapeDtypeStruct((b, N_HEADS, HEAD_DIM), F32),
        compiler_params=_params("parallel"),
        name="attn_sample",
    )(q, kvn, *views)


def _matmul_res_kernel(x_ref, a_ref, w_ref, o_ref):
    o_ref[...] = x_ref[...] + jnp.dot(a_ref[...].astype(BF16), w_ref[...], preferred_element_type=F32)


def _matmul_res(x, a, w):
    vm = pl.BlockSpec(memory_space=pltpu.VMEM)
    return pl.pallas_call(
        _matmul_res_kernel,
        in_specs=[vm] * 3,
        out_specs=vm,
        out_shape=jax.ShapeDtypeStruct(x.shape, F32),
        compiler_params=pltpu.CompilerParams(vmem_limit_bytes=VMEM_LIMIT_BYTES),
        name="matmul_res",
    )(x, a, w)


def kernel(x_prompt, x_sample, state_h, state_conv, cache_kv_g1, cache_kv_g2, cache_kv_g3,
           ffn1_norm, ffn1_w_up, ffn1_w_down, mix_norm, ffn2_norm, ffn2_w_up, ffn2_w_down,
           rg_w_in, rg_conv_w, rg_conv_b, rg_gate_a_w, rg_gate_a_b, rg_gate_x_w, rg_gate_x_b,
           rg_lambda, rg_w_out, kv_norm, w_kv, attn_w_q, attn_w_o, final_norm):
    caches = [cache_kv_g1, cache_kv_g2, cache_kv_g3]
    bp, sp, d = x_prompt.shape
    db, ds, _ = x_sample.shape
    assert bp == 1 and ds == 1 and d == D_MODEL
    assert all(c.shape[1] == w for c, w in zip(caches, WINDOWS))
    past = PAST_LEN

    xp = x_prompt.reshape(sp, d)
    xs = x_sample.reshape(db, d)
    row = lambda v: v.reshape(1, -1)
    bf = lambda w: w.astype(BF16)

    half = ROPE_DIM // 2
    inv = ROPE_THETA ** (-jnp.arange(half, dtype=F32) / half)
    inv_lane = jnp.concatenate([inv, inv, jnp.zeros((LANES - ROPE_DIM,), F32)]).reshape(1, LANES)
    rope_p = _rope_tables(inv_lane, sp, pos0=0, pos_stride=1)
    rope_s = _rope_tables(inv_lane, db, pos0=past, pos_stride=0)
    fg = row(final_norm)

    def ffn_pair(xp, xs, g, w_up, w_down, layer, final=False, next_gain=None, shift=None):
        eg = fg if next_gain is None else row(next_gain)
        kw = dict(layer=layer, final=final, norm_next=next_gain is not None, tf=512)
        xs, w_bf, xns, _ = _ffn(xs, row(g[layer]), (w_up, w_down), eg, tm=db, **kw)
        xp, _, xnp, shifted = _ffn(xp, row(g[layer]), w_bf, eg, tm=512, shift=shift, **kw)
        return xp, xs, xnp, xns, shifted

    t12 = 128
    whole = lambda g: _ShiftJob(caches[g], None, 0, db * (WINDOWS[g] // t12), t12)
    t3 = 256
    nblk3 = db * (WINDOWS[2] // t3)
    shifted3, next_blk = None, 0

    def ride(steps):
        nonlocal next_blk
        count = min(steps, nblk3 - next_blk)
        job = _ShiftJob(caches[2], shifted3, next_blk, count, t3)
        next_blk += count
        return job

    xp, xs, _, _, shifted2 = ffn_pair(xp, xs, ffn1_norm, ffn1_w_up, ffn1_w_down, 0,
                                      shift=whole(1))
    rg = (row(mix_norm[0]), bf(rg_w_in[0]), rg_conv_w[0], row(rg_conv_b[0]), bf(rg_gate_a_w[0]),
          row(rg_gate_a_b[0]), bf(rg_gate_x_w[0]), row(rg_gate_x_b[0]), row(rg_lambda[0]),
          bf(rg_w_out[0]))
    xp, p_h, p_conv, shifted1 = _rg_prompt(xp, *rg, tm=256, shift=whole(0))
    xs, s_h, s_conv = _rg_sample(xs, *rg, state_h[0], state_conv[0].reshape(db, -1))
    xp, xs, xnp, xns, _ = ffn_pair(xp, xs, ffn2_norm, ffn2_w_up, ffn2_w_down, 0,
                                   next_gain=kv_norm)

    wkv = bf(w_kv)
    tm_p = 256
    kv_p, p_kv, kv_s = [], [], []
    for g in range(N_GROUPS):
        common = dict(rope=(True, False), scale=(1.0, 1.0))
        kv, tail, shifted3 = _proj(xnp, rope_p, wkv, (g, N_GROUPS + g), dil=DILATIONS[g],
                                   tm=tm_p, out_dtype=BF16, tail=WINDOWS[g],
                                   shift=ride(sp // tm_p), **common)
        kv_p.append(kv)
        p_kv.append(jnp.swapaxes(tail, 0, 1).reshape(1, WINDOWS[g], 2, N_HEADS, HEAD_DIM))
        kv_s.append(_proj(xns, rope_s, wkv, (g, N_GROUPS + g), dil=1, tm=db, out_dtype=F32,
                          **common)[0])
    kvn = jnp.stack(kv_s).reshape(N_GROUPS, 2, db, N_HEADS, HEAD_DIM)

    xp, xs, xnp, xns, _ = ffn_pair(xp, xs, ffn1_norm, ffn1_w_up, ffn1_w_down, 1,
                                   next_gain=mix_norm[1])
    wq = bf(attn_w_q[0])
    wo = bf(attn_w_o[0])
    qcommon = dict(rope=(True,), scale=(HEAD_DIM ** -0.5,))
    tq = 512
    og, lg, q_s = [], [], []
    for g in range(N_GROUPS):
        q, _, shifted3 = _proj(xnp, rope_p, wq, (g,), dil=DILATIONS[g], tm=tm_p, out_dtype=BF16, shift=ride(sp // tm_p), **qcommon)
        o, lse, shifted3 = _attn_group(q, kv_p[g], tq=tq, shift=ride(sp // tq))
        og.append(o)
        lg.append(lse)
        q_s.append(_proj(xns, rope_s, wq, (g,), dil=1, tm=db,
                         out_dtype=F32, **qcommon)[0])
    xp, shifted3 = _merge_wo(xp, og, lg, wo, tm=tm_p, shift=ride(sp // tm_p))
    assert next_blk == nblk3
    qs = jnp.stack(q_s).reshape(N_GROUPS, db, N_HEADS, HEAD_DIM)
    xs = _matmul_res(xs, _attn_sample(qs, kvn, caches).reshape(db, GW), wo)
    xp, xs, _, _, _ = ffn_pair(xp, xs, ffn2_norm, ffn2_w_up, ffn2_w_down, 1, final=True)

    s_kv = _insert_rows([kvn[g] for g in range(N_GROUPS)], [shifted1, shifted2, shifted3])

    return (xp.reshape(bp, sp, d), xs.reshape(db, ds, d),
            p_h.reshape(1, bp, D_RNN), p_conv.reshape(1, bp, CONV_WIDTH - 1, D_RNN),
            p_kv[0], p_kv[1], p_kv[2],
            s_h.reshape(1, db, D_RNN), s_conv.reshape(1, db, CONV_WIDTH - 1, D_RNN),
            s_kv[0], s_kv[1], s_kv[2])
```

```python
import functools
from typing import NamedTuple, Optional

import jax
import jax.numpy as jnp
from jax import lax
from jax.experimental import pallas as pl
from jax.experimental.pallas import tpu as pltpu

F32 = jnp.float32
BF16 = jnp.bfloat16

D_MODEL = 2048
D_FF = 5632
D_RNN = 2048
N_RG_BLOCKS = 16
RG_BLOCK = D_RNN // N_RG_BLOCKS
CONV_WIDTH = 4
RG_C = 8.0
HEAD_DIM = 128
N_HEADS = 16
N_GROUPS = 3
WINDOWS = (128, 512, 2048)
DILATIONS = (1, 4, 16)
N_KEYS = 129
ROPE_DIM = 32
ROPE_THETA = 500000.0
EPS = 1e-6
PAST_LEN = 8192
GW =N_HEADS * HEAD_DIM

LANES = 128
SUBLANES = 8
VMEM_LIMIT_BYTES = 60 * 1024 * 1024


def _params(*sem):
    return pltpu.CompilerParams(dimension_semantics=sem, vmem_limit_bytes=VMEM_LIMIT_BYTES)


def _rms(x, g):
    var = jnp.mean(x * x, axis=-1, keepdims=True)
    return (x * lax.rsqrt(var + EPS)) * g


def _softplus(z):
    return jnp.maximum(z, 0.0) + jnp.log1p(jnp.exp(-jnp.abs(z)))


def _sigmoid(z):
    return 0.5 * jnp.tanh(0.5 * z) + 0.5


class _ShiftJob(NamedTuple):
    cache: jax.Array
    prev: Optional[jax.Array]
    first: int
    count: int
    t: int


class _Rider:
    def __init__(self, c_ref, nxt_ref, o_ref):
        self.c, self.nxt, self.o = c_ref, nxt_ref, o_ref
        self.t = c_ref.shape[1]

    def part(self, p, n):
        rp = self.t // n
        lo, hi = p * rp, (p + 1) * rp
        if p == n - 1:
            self.o[0, lo:hi - 1] = self.c[0, lo + 1:hi]
            self.o[0, hi - 1] = self.nxt[0, 0]
        else:
            self.o[0, lo:hi] = self.c[0, lo + 1:hi + 1]

    def loop_part(self, it, n):
        rp = self.t // n
        lo = jnp.minimum(it, n - 2) * rp
        self.o[0, pl.ds(lo, rp)] = self.c[0, pl.ds(lo + 1, rp)]


def _ride(rider, p, n):
    if rider is not None:
        rider.part(p, n)


def _with_shift(kernel_fn, n_in, n_out, n_shift_in):
    def wrapped(*refs):
        outs_at = n_in + n_shift_in
        rider = _Rider(refs[n_in], refs[n_in + 1], refs[outs_at + n_out])
        kernel_fn(*refs[:n_in], *refs[outs_at:outs_at + n_out], *refs[outs_at + n_out + 1:],
                  rider=rider)
    return wrapped


def _call(kernel_fn, *, name, grid, in_specs, args, out_specs, out_shape, scratch, sem, shift=None):
    aliases = {}
    if shift is not None:
        b, w = shift.cache.shape[:2]
        nblk = w // shift.t
        tail = shift.cache.shape[2:]
        zeros = (0,) * len(tail)

        def blk(*idx):
            step = idx[0]
            for extent, i in zip(grid[1:], idx[1:]):
                step = step * extent + i
            k = shift.first + jnp.minimum(step, shift.count - 1)
            return k // nblk, k % nblk

        def nxt(*idx):
            bi, ji = blk(*idx)
            return bi, jnp.minimum((ji + 1) * shift.t, w - 1)

        shift_in = [pl.BlockSpec((1, shift.t) + tail, lambda *idx: blk(*idx) + zeros),
                    pl.BlockSpec((1, 1) + tail, lambda *idx: nxt(*idx) + zeros)]
        shift_args = [shift.cache, shift.cache]
        if shift.prev is not None:
            shift_in.append(pl.BlockSpec(memory_space=pl.ANY))
            shift_args.append(shift.prev)
            aliases = {len(in_specs) + 2: len(out_specs)}
        kernel_fn = _with_shift(kernel_fn, len(in_specs), len(out_specs), len(shift_in))
        in_specs = list(in_specs) + shift_in
        args = list(args) + shift_args
        out_specs = list(out_specs) + [
            pl.BlockSpec((1, shift.t) + tail, lambda *idx: blk(*idx) + zeros)]
        out_shape = list(out_shape) + [jax.ShapeDtypeStruct(shift.cache.shape, shift.cache.dtype)]
    sems = (sem,) if isinstance(sem, str) else sem
    return pl.pallas_call(
        kernel_fn, grid=grid, in_specs=in_specs, out_specs=out_specs, out_shape=out_shape,
        input_output_aliases=aliases, scratch_shapes=scratch, compiler_params=_params(*sems),
        name=name,
    )(*args)


def _insert_rows_kernel(*refs):
    n = len(refs) // 3
    for new_ref, o_ref in zip(refs[:n], refs[2 * n:]):
        o_ref[0, 0] = new_ref[:, 0]


def _insert_rows(kv_new, shifted):
    n = len(shifted)
    b = shifted[0].shape[0]
    tail = shifted[0].shape[2:]
    row_spec = lambda w: pl.BlockSpec((1, 1) + tail, lambda i: (i, w - 1, 0, 0, 0))
    return pl.pallas_call(
        _insert_rows_kernel,
        grid=(b,),
        in_specs=[pl.BlockSpec((2, 1) + tail[1:], lambda i: (0, i, 0, 0))] * n
        + [pl.BlockSpec(memory_space=pl.ANY)] * n,
        out_specs=[row_spec(s.shape[1]) for s in shifted],
        out_shape=[jax.ShapeDtypeStruct(s.shape, s.dtype) for s in shifted],
        input_output_aliases={n + k: k for k in range(n)},
        compiler_params=_params("arbitrary"),
        name="insert_rows",
    )(*kv_new, *shifted)


def _ffn_kernel(x_ref, g_ref, wg_ref, wu_ref, wd_ref, fg_ref, o_ref, *rest, nf, final, emit,
                norm_next, rider=None):
    xn_ref = rest[-1]
    j = pl.program_id(1)

    @pl.when(j == 0)
    def _():
        xn_ref[...] = _rms(x_ref[...], g_ref[...]).astype(BF16)
        o_ref[...] = jnp.zeros_like(o_ref)

    wg, wu, wd = wg_ref[...], wu_ref[...], wd_ref[...]
    if emit:
        wg, wu, wd = wg.astype(BF16), wu.astype(BF16), wd.astype(BF16)
        rest[0][...], rest[1][...], rest[2][...] = wg, wu, wd

    xn = xn_ref[...]
    _ride(rider, 0, 4)
    gate = jnp.dot(xn, wg, preferred_element_type=F32)
    _ride(rider, 1, 4)
    up = jnp.dot(xn, wu, preferred_element_type=F32)
    _ride(rider, 2, 4)
    h = (gate * jax.nn.sigmoid(gate)) * up
    _ride(rider, 3, 4)
    o_ref[...] += jnp.dot(h.astype(BF16), wd, preferred_element_type=F32)

    @pl.when(j == nf - 1)
    def _():
        y = x_ref[...] + 0.5 * o_ref[...]
        if final:
            y = _rms(y, fg_ref[...])
        o_ref[...] = y
        if norm_next:
            rest[-2][...] = _rms(y, fg_ref[...]).astype(BF16)


def _ffn(x, g, weights, fg, *, layer, final, tm, tf, norm_next=False, shift=None):
    m, d = x.shape
    assert not (final and norm_next)
    emit = len(weights) == 2
    nf = D_FF // tf
    if emit:
        w_up, w_down = weights
        w_args = [w_up, w_up, w_down]
        w_specs = [
            pl.BlockSpec((None, d, tf), lambda i, j: (layer, 0, j)),
            pl.BlockSpec((None, d, tf), lambda i, j: (layer, 0, j + nf)),
            pl.BlockSpec((None, tf, d), lambda i, j: (layer, j, 0)),
        ]
    else:
        w_args = list(weights)
        w_specs = [None] * 3
    cast_specs = [
        pl.BlockSpec((d, tf), lambda i, j: (0, j)),
        pl.BlockSpec((d, tf), lambda i, j: (0, j)),
        pl.BlockSpec((tf, d), lambda i, j: (j, 0)),
    ]
    cast_shapes = [jax.ShapeDtypeStruct((d, D_FF), BF16)] * 2 + [jax.ShapeDtypeStruct((D_FF, d), BF16)]
    row_spec = pl.BlockSpec((tm, d), lambda i, j: (i, 0))
    extra_specs = (cast_specs if emit else []) + ([row_spec] if norm_next else [])
    extra_shapes = (cast_shapes if emit else []) + (
        [jax.ShapeDtypeStruct((m, d), BF16)] if norm_next else [])
    outs = _call(
        functools.partial(_ffn_kernel, nf=nf, final=final, emit=emit, norm_next=norm_next),
        name="ffn", grid=(m // tm, nf),
        in_specs=[
            pl.BlockSpec((tm, d), lambda i, j: (i, 0)),
            pl.BlockSpec((1, d), lambda i, j: (0, 0)),
            *(w_specs if emit else cast_specs),
            pl.BlockSpec((1, d), lambda i, j: (0, 0)),
        ],
        args=[x, g, *w_args, fg],
        out_specs=[row_spec] + extra_specs,
        out_shape=[jax.ShapeDtypeStruct((m, d), F32)] + extra_shapes,
        scratch=[pltpu.VMEM((tm, d), BF16)], sem=("arbitrary", "arbitrary"), shift=shift)
    n_cast = 3 if emit else 0
    return (outs[0], tuple(outs[1:4]) if emit else None,
            outs[1 + n_cast] if norm_next else None, outs[-1] if shift is not None else None)


def _rg_gates(xc, gaw, gab, gxw, gxb, sp):
    xcb = xc.astype(BF16)
    r = _sigmoid(jnp.dot(xcb, gaw, preferred_element_type=F32) + gab)
    ig = _sigmoid(jnp.dot(xcb, gxw, preferred_element_type=F32) + gxb)
    log_a = (-RG_C * r) * sp
    a = jnp.exp(log_a)
    b = jnp.sqrt(-jnp.tanh(log_a) * (1.0 + a * a)) * (ig * xc)
    return a, b


def _rg_prompt_kernel(x_ref, g_ref, win_ref, cw_ref, cb_ref, gaw_ref, gab_ref, gxw_ref, gxb_ref,
                      lam_ref, wout_ref, o_ref, hlast_ref, conv_ref,
                      ext_ref, gate_ref, y_ref, h_ref, *, tm, rider=None):
    i = pl.program_id(0)
    pad = SUBLANES

    @pl.when(i == 0)
    def _():
        ext_ref[0:pad, :] = jnp.zeros((pad, D_RNN), F32)
        h_ref[...] = jnp.zeros_like(h_ref)

    x = x_ref[...]
    y_ref[...] = _rms(x, g_ref[...]).astype(BF16)
    gate_ref[...] = jnp.dot(y_ref[...], win_ref[:, 0:D_RNN], preferred_element_type=F32)
    ext_ref[pad:pad + tm, :] = jnp.dot(y_ref[...], win_ref[:, D_RNN:2 * D_RNN],
                                       preferred_element_type=F32)

    sub = lax.broadcasted_iota(jnp.int32, (tm // SUBLANES, SUBLANES, RG_BLOCK), 1)
    for n in range(N_RG_BLOCKS):
        cs = slice(n * RG_BLOCK, (n + 1) * RG_BLOCK)
        _ride(rider, n, N_RG_BLOCKS)
        e = ext_ref[:, cs]
        taps = [pltpu.roll(e, CONV_WIDTH - 1 - j, 0)[pad:pad + tm, :]
                for j in range(CONV_WIDTH - 1)] + [e[pad:pad + tm, :]]
        xc = cb_ref[:, cs] + taps[0] * cw_ref[0:1, cs]
        for j in range(1, CONV_WIDTH):
            xc = xc + taps[j] * cw_ref[j:j + 1, cs]
        sp = _softplus(-lam_ref[:, cs])
        a, b = _rg_gates(xc, gaw_ref[n], gab_ref[:, cs], gxw_ref[n], gxb_ref[:, cs], sp)
        a = a.reshape(tm // SUBLANES, SUBLANES, RG_BLOCK)
        b = b.reshape(tm // SUBLANES, SUBLANES, RG_BLOCK)
        s = 1
        while s < SUBLANES:
            keep = sub >= s
            b = b + a * jnp.where(keep, pltpu.roll(b, s, 1), 0.0)
            a = a * jnp.where(keep, pltpu.roll(a, s, 1), 1.0)
            s *= 2
        carry = jnp.broadcast_to(h_ref[:, cs], (SUBLANES, RG_BLOCK))
        hs = []
        for k in range(tm // SUBLANES):
            hk = a[k] * carry + b[k]
            hs.append(hk)
            carry = jnp.broadcast_to(hk[SUBLANES - 1:SUBLANES, :], (SUBLANES, RG_BLOCK))
        h = jnp.concatenate(hs, axis=0)
        h_ref[:, cs] = carry[0:1, :]
        y_ref[:, cs] = (h * jax.nn.gelu(gate_ref[:, cs])).astype(BF16)

    o_ref[...] = x + jnp.dot(y_ref[...], wout_ref[...], preferred_element_type=F32)
    hlast_ref[...] = h_ref[...]
    conv_ref[...] = ext_ref[pad + tm - 3:pad + tm, :]
    ext_ref[0:pad, :] = ext_ref[tm:tm + pad, :]


def _rg_prompt(x, g, w_in, cw, cb, gaw, gab, gxw, gxb, lam, w_out, *, tm, shift=None):
    m, d = x.shape
    const2 = lambda i: (0, 0)
    const3 = lambda i: (0, 0, 0)
    one = pl.Buffered(1)
    return _call(
        functools.partial(_rg_prompt_kernel, tm=tm),
        name="rg_prompt", grid=(m // tm,),
        in_specs=[
            pl.BlockSpec((tm, d), lambda i: (i, 0)),
            pl.BlockSpec((1, d), const2),
            pl.BlockSpec((d, 2 * D_RNN), const2, pipeline_mode=one),
            pl.BlockSpec((CONV_WIDTH, D_RNN), const2),
            pl.BlockSpec((1, D_RNN), const2),
            pl.BlockSpec((N_RG_BLOCKS, RG_BLOCK, RG_BLOCK), const3),
            pl.BlockSpec((1, D_RNN), const2),
            pl.BlockSpec((N_RG_BLOCKS, RG_BLOCK, RG_BLOCK), const3),
            pl.BlockSpec((1, D_RNN), const2),
            pl.BlockSpec((1, D_RNN), const2),
            pl.BlockSpec((D_RNN, d), const2, pipeline_mode=one),
        ],
        args=[x, g, w_in, cw, cb, gaw, gab, gxw, gxb, lam, w_out],
        out_specs=[
            pl.BlockSpec((tm, d), lambda i: (i, 0)),
            pl.BlockSpec((1, D_RNN), const2),
            pl.BlockSpec((CONV_WIDTH - 1, D_RNN), const2),
        ],
        out_shape=[
            jax.ShapeDtypeStruct((m, d), F32),
            jax.ShapeDtypeStruct((1, D_RNN), F32),
            jax.ShapeDtypeStruct((CONV_WIDTH - 1, D_RNN), F32),
        ],
        scratch=[
            pltpu.VMEM((tm + SUBLANES, D_RNN), F32),
            pltpu.VMEM((tm, D_RNN), F32),
            pltpu.VMEM((tm, D_RNN), BF16),
            pltpu.VMEM((1, D_RNN), F32),
        ],
        sem="arbitrary", shift=shift)


def _rg_sample_kernel(x_ref, g_ref, win_ref, cw_ref, cb_ref, gaw_ref, gab_ref, gxw_ref, gxb_ref,
                      lam_ref, wout_ref, h0_ref, c0_ref, o_ref, hnew_ref, cnew_ref, y_ref):
    x = x_ref[...]
    xn = _rms(x, g_ref[...]).astype(BF16)
    gate = jnp.dot(xn, win_ref[:, 0:D_RNN], preferred_element_type=F32)
    rec = jnp.dot(xn, win_ref[:, D_RNN:2 * D_RNN], preferred_element_type=F32)
    for n in range(N_RG_BLOCKS):
        cs = slice(n * RG_BLOCK, (n + 1) * RG_BLOCK)
        taps = [c0_ref[:, j * D_RNN + n * RG_BLOCK:j * D_RNN + (n + 1) * RG_BLOCK]
                for j in range(CONV_WIDTH - 1)] + [rec[:, cs]]
        xc = cb_ref[:, cs] + taps[0] * cw_ref[0:1, cs]
        for j in range(1, CONV_WIDTH):
            xc = xc + taps[j] * cw_ref[j:j + 1, cs]
        sp = _softplus(-lam_ref[:, cs])
        a, b = _rg_gates(xc, gaw_ref[n], gab_ref[:, cs], gxw_ref[n], gxb_ref[:, cs], sp)
        h = a * h0_ref[:, cs] + b
        hnew_ref[:, cs] = h
        y_ref[:, cs] = (h * jax.nn.gelu(gate[:, cs])).astype(BF16)
    o_ref[...] = x + jnp.dot(y_ref[...], wout_ref[...], preferred_element_type=F32)
    cnew_ref[:, 0:2 * D_RNN] = c0_ref[:, D_RNN:3 * D_RNN]
    cnew_ref[:, 2 * D_RNN:3 * D_RNN] = rec


def _rg_sample(x, g, w_in, cw, cb, gaw, gab, gxw, gxb, lam, w_out, h0, c0):
    m, d = x.shape
    vm = pl.BlockSpec(memory_space=pltpu.VMEM)
    return pl.pallas_call(
        _rg_sample_kernel,
        in_specs=[vm] * 13,
        out_specs=[vm] * 3,
        out_shape=[
            jax.ShapeDtypeStruct((m, d), F32),
            jax.ShapeDtypeStruct((m, D_RNN), F32),
            jax.ShapeDtypeStruct((m, (CONV_WIDTH - 1) * D_RNN), F32),
        ],
        scratch_shapes=[pltpu.VMEM((m, D_RNN), BF16)],
        compiler_params=pltpu.CompilerParams(vmem_limit_bytes=VMEM_LIMIT_BYTES),
        name="rg_sample",
    )(x, g, w_in, cw, cb, gaw, gab, gxw, gxb, lam, w_out, h0, c0)


_PROJ_CHUNK = 512


def _rope_kernel(inv_ref, cos_ref, shi_ref, slo_ref, *, tr, pos0, pos_stride):
    half = ROPE_DIM // 2
    r = lax.broadcasted_iota(jnp.int32, (tr, LANES), 0)
    pos = pos0 + (pl.program_id(0) * tr + r) * pos_stride
    ang = pos.astype(F32) * inv_ref[...]
    lane = lax.broadcasted_iota(jnp.int32, (tr, LANES), 1)
    sin = jnp.sin(ang)
    cos_ref[...] = jnp.cos(ang)
    shi_ref[...] = jnp.where((lane >= half) & (lane < ROPE_DIM), sin, 0.0)
    slo_ref[...] = jnp.where(lane < half, -sin, 0.0)


def _rope_tables(inv_lane, n_rows, *, pos0, pos_stride):
    tr = min(n_rows, 1024)
    spec = pl.BlockSpec((tr, LANES), lambda i: (i, 0))
    return pl.pallas_call(
        functools.partial(_rope_kernel, tr=tr, pos0=pos0, pos_stride=pos_stride),
        grid=(n_rows // tr,),
        in_specs=[pl.BlockSpec((1, LANES), lambda i: (0, 0))],
        out_specs=[spec] * 3,
        out_shape=[jax.ShapeDtypeStruct((n_rows, LANES), F32)] * 3,
        compiler_params=_params("parallel"),
        name="rope_tables",
    )(inv_lane)


def _proj_kernel(*refs, tm, dil, n_mats, rope, scale, out_dtype, has_tail, rider=None):
    xn_ref, cos_ref, shi_ref, slo_ref = refs[:4]
    w_refs = refs[4:4 + n_mats]
    o_ref = refs[4 + n_mats]
    tail_ref = refs[5 + n_mats] if has_tail else None
    slab_ref = refs[-1]
    half = ROPE_DIM // 2
    th = tm

    for t0 in range(0, tm, th):
        rows = slice(t0, t0 + th)
        urows = slice(t0 // dil, (t0 + th) // dil)
        cos, s_hi, s_lo = cos_ref[rows, :], shi_ref[rows, :], slo_ref[rows, :]

        n_chunks = GW // _PROJ_CHUNK
        for k in range(n_mats):
            for c0 in range(0, GW, _PROJ_CHUNK):
                _ride(rider, k * n_chunks + c0 // _PROJ_CHUNK, n_mats * n_chunks)
                acc = jnp.dot(xn_ref[rows, :], w_refs[k][:, c0:c0 + _PROJ_CHUNK],
                              preferred_element_type=F32)
                for c in range(c0, c0 + _PROJ_CHUNK, LANES):
                    a = acc[:, c - c0:c - c0 + LANES]
                    if rope[k]:
                        a = (a * cos + pltpu.roll(a, half, 1) * s_hi
                             + pltpu.roll(a, LANES - half, 1) * s_lo)
                    if scale[k] != 1.0:
                        a = a * scale[k]
                    if has_tail:
                        tail_ref[k, rows, c:c + LANES] = a
                    if dil == 1:
                        o_ref[k, 0, rows, c:c + LANES] = a.astype(out_dtype)
                    else:
                        s = c // LANES % slab_ref.shape[0]
                        slab_ref[s, 0:th, :] = a
                        for r_ in range(dil):
                            o_ref[k, r_, urows, c:c + LANES] = (
                                slab_ref[s, pl.ds(r_, th // dil, stride=dil), :].astype(out_dtype))


def _proj(xn, tables, w, col_blocks, *, dil, rope, scale, tm, out_dtype, tail=0, shift=None):
    m, d = xn.shape
    n_mats = len(col_blocks)
    assert m % tm == 0 and tm % dil == 0
    lrows = m // dil
    tail_rows = max(tail, tm) if tail else 0
    i0 = (m - tail_rows) // tm if tail else 0
    in_specs = [pl.BlockSpec((tm, d), lambda i: (i, 0))] + [
        pl.BlockSpec((tm, LANES), lambda i: (i, 0))] * 3
    for cb in col_blocks:
        in_specs.append(pl.BlockSpec((d, GW), lambda i, cb=cb: (0, cb), pipeline_mode=pl.Buffered(1)))
    out_specs = [pl.BlockSpec((n_mats, dil, tm // dil, GW), lambda i: (0, 0, i, 0))]
    out_shape = [jax.ShapeDtypeStruct((n_mats, dil, lrows, GW), out_dtype)]
    if tail:
        out_specs.append(pl.BlockSpec((n_mats, tm, GW), lambda i: (0, jnp.maximum(i - i0, 0), 0)))
        out_shape.append(jax.ShapeDtypeStruct((n_mats, tail_rows, GW), F32))
    outs = _call(
        functools.partial(_proj_kernel, tm=tm, dil=dil, n_mats=n_mats, rope=tuple(rope),
                          scale=tuple(scale), out_dtype=out_dtype, has_tail=bool(tail)),
        name="proj", grid=(m // tm,), in_specs=in_specs, args=[xn, *tables] + [w] * n_mats,
        out_specs=out_specs, out_shape=out_shape, scratch=[pltpu.VMEM((2, tm, LANES), F32)],
        sem="arbitrary", shift=shift)
    return (outs[0], outs[1][:, tail_rows - tail:] if tail else None,
            outs[-1] if shift is not None else None)


_ATTN_HEADS_PER_ITER = 4


def _attn_kernel(q_ref, kc_ref, kp_ref, vc_ref, vp_ref, o_ref, lse_ref, *, tq, nb, rider=None):
    ub = pl.program_id(0) % nb
    blk = LANES
    nsub = tq // blk
    row = lax.broadcasted_iota(jnp.int32, (blk, blk), 0)
    col = lax.broadcasted_iota(jnp.int32, (blk, blk), 1)
    lane = lax.broadcasted_iota(jnp.int32, (blk, LANES), 1)
    cur_ok = col <= row
    prev_tri = col >= row
    neg = -jnp.inf
    dn = (((1,), (1,)), ((), ()))

    first_ok = jnp.logical_and(prev_tri, ub > 0)
    lse_ref[...] = jnp.zeros_like(lse_ref)

    rows = [slice(c * blk, (c + 1) * blk) for c in range(nsub)]
    units = [(e, c) for e in range(_ATTN_HEADS_PER_ITER) for c in range(nsub)]

    n_iter = N_HEADS // _ATTN_HEADS_PER_ITER

    def heads(it, carry):
        if rider is not None:
            rider.loop_part(it, n_iter)
        hn = [it * _ATTN_HEADS_PER_ITER + e for e in range(_ATTN_HEADS_PER_ITER)]
        hs = [pl.ds(pl.multiple_of(h * HEAD_DIM, HEAD_DIM), HEAD_DIM) for h in hn]
        ks = [[kp_ref[:, s]] + [kc_ref[rs, s] for rs in rows] for s in hs]
        vs = [[vp_ref[:, s]] + [vc_ref[rs, s] for rs in rows] for s in hs]
        scores = {}
        for e, c in units:
            q = q_ref[rows[c], hs[e]]
            prev_ok = prev_tri if c > 0 else first_ok
            s_p = jnp.where(prev_ok, lax.dot_general(q, ks[e][c], dn, preferred_element_type=F32), neg)
            s_c = jnp.where(cur_ok, lax.dot_general(q, ks[e][c + 1], dn, preferred_element_type=F32), neg)
            scores[e, c] = (s_p, s_c)
        probs = {}
        lse_new = [lse_ref[rs, :] for rs in rows]
        for e, c in units:
            s_p, s_c = scores[e, c]
            m = jnp.max(jnp.maximum(s_p, s_c), axis=1, keepdims=True)
            p_p = jnp.exp(s_p - m)
            p_c = jnp.exp(s_c - m)
            den = jnp.sum(p_p + p_c, axis=1, keepdims=True)
            lse_new[c] = jnp.where(lane == hn[e], m + jnp.log(den), lse_new[c])
            probs[e, c] = (p_p.astype(BF16), p_c.astype(BF16), 1.0 / den)
        for c, rs in enumerate(rows):
            lse_ref[rs, :] = lse_new[c]
        for e, c in units:
            p_p, p_c, rden = probs[e, c]
            o = (jnp.dot(p_p, vs[e][c], preferred_element_type=F32)
                 + jnp.dot(p_c, vs[e][c + 1], preferred_element_type=F32))
            o_ref[rows[c], hs[e]] = o * rden
        return carry

    lax.fori_loop(0, n_iter, heads, 0)
    _ride(rider, n_iter - 1, n_iter)


def _attn_group(q, kv, *, tq, shift=None):
    _, dil, lrows, _ = q.shape
    s = dil * lrows
    nb = lrows // tq
    sub = tq // LANES
    qv = q.reshape(s, GW)
    kvv = kv.reshape(2, s, GW)
    prev_idx = lambda i: jnp.maximum(i * sub - 1, 0)
    outs = _call(
        functools.partial(_attn_kernel, tq=tq, nb=nb),
        name="attn_group", grid=(s // tq,),
        in_specs=[
            pl.BlockSpec((tq, GW), lambda i: (i, 0)),
            pl.BlockSpec((None, tq, GW), lambda i: (0, i, 0)),
            pl.BlockSpec((None, LANES, GW), lambda i: (0, prev_idx(i), 0)),
            pl.BlockSpec((None, tq, GW), lambda i: (1, i, 0)),
            pl.BlockSpec((None, LANES, GW), lambda i: (1, prev_idx(i), 0)),
        ],
        args=[qv, kvv, kvv, kvv, kvv],
        out_specs=[
            pl.BlockSpec((tq, GW), lambda i: (i, 0)),
            pl.BlockSpec((tq, LANES), lambda i: (i, 0)),
        ],
        out_shape=[
            jax.ShapeDtypeStruct((s, GW), F32),
            jax.ShapeDtypeStruct((s, LANES), F32),
        ],
        scratch=[], sem="arbitrary", shift=shift)
    return (outs[0].reshape(dil, lrows, GW), outs[1].reshape(dil, lrows, LANES),
            outs[-1] if shift is not None else None)


def _merge_wo_kernel(x_ref, o1_ref, o2_ref, o3_ref, l1_ref, l2_ref, l3_ref, w_ref, out_ref,
                     onat_ref, lnat_ref, a_ref, *, tm, rider=None):
    o_refs = (o1_ref, o2_ref, o3_ref)
    l_refs = (l1_ref, l2_ref, l3_ref)
    for g, dil in enumerate(DILATIONS):
        rows = tm // dil
        for r in range(dil):
            dst = pl.ds(r, rows, stride=dil) if dil > 1 else slice(None)
            lnat_ref[g, dst, :] = l_refs[g][r]
            for h in range(N_HEADS):
                onat_ref[g * N_HEADS + h, dst, :] = o_refs[g][r, :, h * HEAD_DIM:(h + 1) * HEAD_DIM]
    l1, l2, l3 = lnat_ref[0], lnat_ref[1], lnat_ref[2]
    mx = jnp.maximum(jnp.maximum(l1, l2), l3)
    e1, e2, e3 = jnp.exp(l1 - mx), jnp.exp(l2 - mx), jnp.exp(l3 - mx)
    tot = (e1 + e2) + e3
    w1, w2, w3 = e1 / tot, e2 / tot, e3 / tot
    for h in range(N_HEADS):
        _ride(rider, h, N_HEADS)
        a = onat_ref[h] * w1[:, h:h + 1]
        a = a + onat_ref[N_HEADS + h] * w2[:, h:h + 1]
        a = a + onat_ref[2 * N_HEADS + h] * w3[:, h:h + 1]
        a_ref[:, h * HEAD_DIM:(h + 1) * HEAD_DIM] = a.astype(BF16)
    out_ref[...] = x_ref[...] + jnp.dot(a_ref[...], w_ref[...], preferred_element_type=F32)


def _merge_wo(x, os_, ls_, w_o, *, tm, shift=None):
    m, d = x.shape
    rowblk = lambda i: (i, 0)
    deint = lambda i: (0, i, 0)
    in_specs = [pl.BlockSpec((tm, d), rowblk)]
    in_specs += [pl.BlockSpec((dil, tm // dil, GW), deint) for dil in DILATIONS]
    in_specs += [pl.BlockSpec((dil, tm // dil, LANES), deint) for dil in DILATIONS]
    in_specs += [pl.BlockSpec((GW, d), lambda i: (0, 0), pipeline_mode=pl.Buffered(1))]
    outs = _call(
        functools.partial(_merge_wo_kernel, tm=tm),
        name="merge_wo", grid=(m // tm,), in_specs=in_specs, args=[x, *os_, *ls_, w_o],
        out_specs=[pl.BlockSpec((tm, d), rowblk)],
        out_shape=[jax.ShapeDtypeStruct((m, d), F32)],
        scratch=[pltpu.VMEM((N_GROUPS * N_HEADS, tm, LANES), F32),
                 pltpu.VMEM((N_GROUPS, tm, LANES), F32),
                 pltpu.VMEM((tm, GW), BF16)],
        sem="arbitrary", shift=shift)
    return outs[0], (outs[-1] if shift is not None else None)


def _bf16_round(v):
    return v.astype(BF16).astype(F32)


def _attn_sample_kernel(q_ref, kvn_ref, c1_ref, c2_ref, c3_ref, o_ref):
    outs, lses = [], []
    for g, c_ref in enumerate((c1_ref, c2_ref, c3_ref)):
        q = _bf16_round(q_ref[g, 0])
        kn = _bf16_round(kvn_ref[g, 0, 0])
        vn = _bf16_round(kvn_ref[g, 1, 0])
        kc = _bf16_round(c_ref[0, :, 0, 0])
        vc = _bf16_round(c_ref[0, :, 0, 1])
        s_c = jnp.sum(kc * q[None], axis=-1, keepdims=True)
        s_n = jnp.sum(kn * q, axis=-1, keepdims=True)
        m = jnp.maximum(jnp.max(s_c, axis=0), s_n)
        p_c = jnp.exp(s_c - m[None])
        p_n = jnp.exp(s_n - m)
        den = jnp.sum(p_c, axis=0) + p_n
        o = jnp.sum(_bf16_round(p_c / den[None]) * vc, axis=0) + _bf16_round(p_n / den) * vn
        outs.append(o)
        lses.append(m + jnp.log(den))
    mx = jnp.maximum(jnp.maximum(lses[0], lses[1]), lses[2])
    es = [jnp.exp(l - mx) for l in lses]
    tot = (es[0] + es[1]) + es[2]
    acc = outs[0] * (es[0] / tot)
    acc = acc + outs[1] * (es[1] / tot)
    acc = acc + outs[2] * (es[2] / tot)
    o_ref[0] = acc


def _attn_sample(q, kvn, caches):
    b = q.shape[1]
    nk = N_KEYS - 1
    views = [c.reshape(b, nk, DILATIONS[g], 2, N_HEADS, HEAD_DIM) for g, c in enumerate(caches)]
    cspec = pl.BlockSpec((1, nk, 1, 2, N_HEADS, HEAD_DIM), lambda i: (i, 0, 0, 0, 0, 0))
    return pl.pallas_call(
        _attn_sample_kernel,
        grid=(b,),
        in_specs=[
            pl.BlockSpec((N_GROUPS, 1, N_HEADS, HEAD_DIM), lambda i: (0, i, 0, 0)),
            pl.BlockSpec((N_GROUPS, 2, 1, N_HEADS, HEAD_DIM), lambda i: (0, 0, i, 0, 0)),
            cspec, cspec, cspec,
        ],
        out_specs=pl.BlockSpec((1, N_HEADS, HEAD_DIM), lambda i: (i, 0, 0)),
        out_shape=jax.ShapeDtypeStruct((b, N_HEADS, HEAD_DIM), F32),
        compiler_params=_params("parallel"),
        name="attn_sample",
    )(q, kvn, *views)


def _matmul_res_kernel(x_ref, a_ref, w_ref, o_ref):
    o_ref[...] = x_ref[...] + jnp.dot(a_ref[...].astype(BF16), w_ref[...], preferred_element_type=F32)


def _matmul_res(x, a, w):
    vm = pl.BlockSpec(memory_space=pltpu.VMEM)
    return pl.pallas_call(
        _matmul_res_kernel,
        in_specs=[vm] * 3,
        out_specs=vm,
        out_shape=jax.ShapeDtypeStruct(x.shape, F32),
        compiler_params=pltpu.CompilerParams(vmem_limit_bytes=VMEM_LIMIT_BYTES),
        name="matmul_res",
    )(x, a, w)


def kernel(x_prompt, x_sample, state_h, state_conv, cache_kv_g1, cache_kv_g2, cache_kv_g3,
           ffn1_norm, ffn1_w_up, ffn1_w_down, mix_norm, ffn2_norm, ffn2_w_up, ffn2_w_down,
           rg_w_in, rg_conv_w, rg_conv_b, rg_gate_a_w, rg_gate_a_b, rg_gate_x_w, rg_gate_x_b,
           rg_lambda, rg_w_out, kv_norm, w_kv, attn_w_q, attn_w_o, final_norm):
    caches = [cache_kv_g1, cache_kv_g2, cache_kv_g3]
    bp, sp, d = x_prompt.shape
    db, ds, _ = x_sample.shape
    assert bp == 1 and ds == 1 and d == D_MODEL
    assert all(c.shape[1] == w for c, w in zip(caches, WINDOWS))
    past = PAST_LEN

    xp = x_prompt.reshape(sp, d)
    xs = x_sample.reshape(db, d)
    row = lambda v: v.reshape(1, -1)
    bf = lambda w: w.astype(BF16)

    half = ROPE_DIM // 2
    inv = ROPE_THETA ** (-jnp.arange(half, dtype=F32) / half)
    inv_lane = jnp.concatenate([inv, inv, jnp.zeros((LANES - ROPE_DIM,), F32)]).reshape(1, LANES)
    rope_p = _rope_tables(inv_lane, sp, pos0=0, pos_stride=1)
    rope_s = _rope_tables(inv_lane, db, pos0=past, pos_stride=0)
    fg = row(final_norm)

    def ffn_pair(xp, xs, g, w_up, w_down, layer, final=False, next_gain=None, shift=None):
        eg = fg if next_gain is None else row(next_gain)
        kw = dict(layer=layer, final=final, norm_next=next_gain is not None, tf=512)
        xs, w_bf, xns, _ = _ffn(xs, row(g[layer]), (w_up, w_down), eg, tm=db, **kw)
        xp, _, xnp, shifted = _ffn(xp, row(g[layer]), w_bf, eg, tm=512, shift=shift, **kw)
        return xp, xs, xnp, xns, shifted

    t12 = 128
    whole = lambda g: _ShiftJob(caches[g], None, 0, db * (WINDOWS[g] // t12), t12)
    t3 = 256
    nblk3 = db * (WINDOWS[2] // t3)
    shifted3, next_blk = None, 0

    def ride(steps):
        nonlocal next_blk
        count = min(steps, nblk3 - next_blk)
        job = _ShiftJob(caches[2], shifted3, next_blk, count, t3)
        next_blk += count
        return job

    xp, xs, _, _, shifted2 = ffn_pair(xp, xs, ffn1_norm, ffn1_w_up, ffn1_w_down, 0,
                                      shift=whole(1))
    rg = (row(mix_norm[0]), bf(rg_w_in[0]), rg_conv_w[0], row(rg_conv_b[0]), bf(rg_gate_a_w[0]),
          row(rg_gate_a_b[0]), bf(rg_gate_x_w[0]), row(rg_gate_x_b[0]), row(rg_lambda[0]),
          bf(rg_w_out[0]))
    xp, p_h, p_conv, shifted1 = _rg_prompt(xp, *rg, tm=256, shift=whole(0))
    xs, s_h, s_conv = _rg_sample(xs, *rg, state_h[0], state_conv[0].reshape(db, -1))
    xp, xs, xnp, xns, _ = ffn_pair(xp, xs, ffn2_norm, ffn2_w_up, ffn2_w_down, 0,
                                   next_gain=kv_norm)

    wkv = bf(w_kv)
    tm_p = 256
    kv_p, p_kv, kv_s = [], [], []
    for g in range(N_GROUPS):
        common = dict(rope=(True, False), scale=(1.0, 1.0))
        kv, tail, shifted3 = _proj(xnp, rope_p, wkv, (g, N_GROUPS + g), dil=DILATIONS[g],
                                   tm=tm_p, out_dtype=BF16, tail=WINDOWS[g],
                                   shift=ride(sp // tm_p), **common)
        kv_p.append(kv)
        p_kv.append(jnp.swapaxes(tail, 0, 1).reshape(1, WINDOWS[g], 2, N_HEADS, HEAD_DIM))
        kv_s.append(_proj(xns, rope_s, wkv, (g, N_GROUPS + g), dil=1, tm=db, out_dtype=F32,
                          **common)[0])
    kvn = jnp.stack(kv_s).reshape(N_GROUPS, 2, db, N_HEADS, HEAD_DIM)

    xp, xs, xnp, xns, _ = ffn_pair(xp, xs, ffn1_norm, ffn1_w_up, ffn1_w_down, 1,
                                   next_gain=mix_norm[1])
    wq = bf(attn_w_q[0])
    wo = bf(attn_w_o[0])
    qcommon = dict(rope=(True,), scale=(HEAD_DIM ** -0.5,))
    tq = 512
    og, lg, q_s = [], [], []
    for g in range(N_GROUPS):
        q, _, shifted3 = _proj(xnp, rope_p, wq, (g,), dil=DILATIONS[g], tm=tm_p, out_dtype=BF16, shift=ride(sp // tm_p), **qcommon)
        o, lse, shifted3 = _attn_group(q, kv_p[g], tq=tq, shift=ride(sp // tq))
        og.append(o)
        lg.append(lse)
        q_s.append(_proj(xns, rope_s, wq, (g,), dil=1, tm=db,
                         out_dtype=F32, **qcommon)[0])
    xp, shifted3 = _merge_wo(xp, og, lg, wo, tm=tm_p, shift=ride(sp // tm_p))
    assert next_blk == nblk3
    qs = jnp.stack(q_s).reshape(N_GROUPS, db, N_HEADS, HEAD_DIM)
    xs = _matmul_res(xs, _attn_sample(qs, kvn, caches).reshape(db, GW), wo)
    xp, xs, _, _, _ = ffn_pair(xp, xs, ffn2_norm, ffn2_w_up, ffn2_w_down, 1, final=True)

    s_kv = _insert_rows([kvn[g] for g in range(N_GROUPS)], [shifted1, shifted2, shifted3])

    return (xp.reshape(bp, sp, d), xs.reshape(db, ds, d),
            p_h.reshape(1, bp, D_RNN), p_conv.reshape(1, bp, CONV_WIDTH - 1, D_RNN),
            p_kv[0], p_kv[1], p_kv[2],
            s_h.reshape(1, db, D_RNN), s_conv.reshape(1, db, CONV_WIDTH - 1, D_RNN),
            s_kv[0], s_kv[1], s_kv[2])
```

```python
import functools
from typing import NamedTuple, Optional

import jax
import jax.numpy as jnp
from jax import lax
from jax.experimental import pallas as pl
from jax.experimental.pallas import tpu as pltpu

F32 = jnp.float32
BF16 = jnp.bfloat16

D_MODEL = 2048
D_FF = 5632
D_RNN = 2048
N_RG_BLOCKS = 16
RG_BLOCK = D_RNN // N_RG_BLOCKS
CONV_WIDTH = 4
RG_C = 8.0
HEAD_DIM = 128
N_HEADS = 16
N_GROUPS = 3
WINDOWS = (128, 512, 2048)
DILATIONS = (1, 4, 16)
N_KEYS = 129
ROPE_DIM = 32
ROPE_THETA = 500000.0
EPS = 1e-6
PAST_LEN = 8192
GW =N_HEADS * HEAD_DIM

LANES = 128
SUBLANES = 8
VMEM_LIMIT_BYTES = 60 * 1024 * 1024


def _params(*sem):
    return pltpu.CompilerParams(dimension_semantics=sem, vmem_limit_bytes=VMEM_LIMIT_BYTES)


def _rms(x, g):
    var = jnp.mean(x * x, axis=-1, keepdims=True)
    return (x * lax.rsqrt(var + EPS)) * g


def _softplus(z):
    return jnp.maximum(z, 0.0) + jnp.log1p(jnp.exp(-jnp.abs(z)))


def _sigmoid(z):
    return 0.5 * jnp.tanh(0.5 * z) + 0.5


class _ShiftJob(NamedTuple):
    cache: jax.Array
    prev: Optional[jax.Array]
    first: int
    count: int
    t: int


class _Rider:
    def __init__(self, c_ref, nxt_ref, o_ref):
        self.c, self.nxt, self.o = c_ref, nxt_ref, o_ref
        self.t = c_ref.shape[1]

    def part(self, p, n):
        rp = self.t // n
        lo, hi = p * rp, (p + 1) * rp
        if p == n - 1:
            self.o[0, lo:hi - 1] = self.c[0, lo + 1:hi]
            self.o[0, hi - 1] = self.nxt[0, 0]
        else:
            self.o[0, lo:hi] = self.c[0, lo + 1:hi + 1]

    def loop_part(self, it, n):
        rp = self.t // n
        lo = jnp.minimum(it, n - 2) * rp
        self.o[0, pl.ds(lo, rp)] = self.c[0, pl.ds(lo + 1, rp)]


def _ride(rider, p, n):
    if rider is not None:
        rider.part(p, n)


def _with_shift(kernel_fn, n_in, n_out, n_shift_in):
    def wrapped(*refs):
        outs_at = n_in + n_shift_in
        rider = _Rider(refs[n_in], refs[n_in + 1], refs[outs_at + n_out])
        kernel_fn(*refs[:n_in], *refs[outs_at:outs_at + n_out], *refs[outs_at + n_out + 1:],
                  rider=rider)
    return wrapped


def _call(kernel_fn, *, name, grid, in_specs, args, out_specs, out_shape, scratch, sem, shift=None):
    aliases = {}
    if shift is not None:
        b, w = shift.cache.shape[:2]
        nblk = w // shift.t
        tail = shift.cache.shape[2:]
        zeros = (0,) * len(tail)

        def blk(*idx):
            step = idx[0]
            for extent, i in zip(grid[1:], idx[1:]):
                step = step * extent + i
            k = shift.first + jnp.minimum(step, shift.count - 1)
            return k // nblk, k % nblk

        def nxt(*idx):
            bi, ji = blk(*idx)
            return bi, jnp.minimum((ji + 1) * shift.t, w - 1)

        shift_in = [pl.BlockSpec((1, shift.t) + tail, lambda *idx: blk(*idx) + zeros),
                    pl.BlockSpec((1, 1) + tail, lambda *idx: nxt(*idx) + zeros)]
        shift_args = [shift.cache, shift.cache]
        if shift.prev is not None:
            shift_in.append(pl.BlockSpec(memory_space=pl.ANY))
            shift_args.append(shift.prev)
            aliases = {len(in_specs) + 2: len(out_specs)}
        kernel_fn = _with_shift(kernel_fn, len(in_specs), len(out_specs), len(shift_in))
        in_specs = list(in_specs) + shift_in
        args = list(args) + shift_args
        out_specs = list(out_specs) + [
            pl.BlockSpec((1, shift.t) + tail, lambda *idx: blk(*idx) + zeros)]
        out_shape = list(out_shape) + [jax.ShapeDtypeStruct(shift.cache.shape, shift.cache.dtype)]
    sems = (sem,) if isinstance(sem, str) else sem
    return pl.pallas_call(
        kernel_fn, grid=grid, in_specs=in_specs, out_specs=out_specs, out_shape=out_shape,
        input_output_aliases=aliases, scratch_shapes=scratch, compiler_params=_params(*sems),
        name=name,
    )(*args)


def _insert_rows_kernel(*refs):
    n = len(refs) // 3
    for new_ref, o_ref in zip(refs[:n], refs[2 * n:]):
        o_ref[0, 0] = new_ref[:, 0]


def _insert_rows(kv_new, shifted):
    n = len(shifted)
    b = shifted[0].shape[0]
    tail = shifted[0].shape[2:]
    row_spec = lambda w: pl.BlockSpec((1, 1) + tail, lambda i: (i, w - 1, 0, 0, 0))
    return pl.pallas_call(
        _insert_rows_kernel,
        grid=(b,),
        in_specs=[pl.BlockSpec((2, 1) + tail[1:], lambda i: (0, i, 0, 0))] * n
        + [pl.BlockSpec(memory_space=pl.ANY)] * n,
        out_specs=[row_spec(s.shape[1]) for s in shifted],
        out_shape=[jax.ShapeDtypeStruct(s.shape, s.dtype) for s in shifted],
        input_output_aliases={n + k: k for k in range(n)},
        compiler_params=_params("arbitrary"),
        name="insert_rows",
    )(*kv_new, *shifted)


def _ffn_kernel(x_ref, g_ref, wg_ref, wu_ref, wd_ref, fg_ref, o_ref, *rest, nf, final, emit,
                norm_next, rider=None):
    xn_ref = rest[-1]
    j = pl.program_id(1)

    @pl.when(j == 0)
    def _():
        xn_ref[...] = _rms(x_ref[...], g_ref[...]).astype(BF16)
        o_ref[...] = jnp.zeros_like(o_ref)

    wg, wu, wd = wg_ref[...], wu_ref[...], wd_ref[...]
    if emit:
        wg, wu, wd = wg.astype(BF16), wu.astype(BF16), wd.astype(BF16)
        rest[0][...], rest[1][...], rest[2][...] = wg, wu, wd

    xn = xn_ref[...]
    _ride(rider, 0, 4)
    gate = jnp.dot(xn, wg, preferred_element_type=F32)
    _ride(rider, 1, 4)
    up = jnp.dot(xn, wu, preferred_element_type=F32)
    _ride(rider, 2, 4)
    h = (gate * jax.nn.sigmoid(gate)) * up
    _ride(rider, 3, 4)
    o_ref[...] += jnp.dot(h.astype(BF16), wd, preferred_element_type=F32)

    @pl.when(j == nf - 1)
    def _():
        y = x_ref[...] + 0.5 * o_ref[...]
        if final:
            y = _rms(y, fg_ref[...])
        o_ref[...] = y
        if norm_next:
            rest[-2][...] = _rms(y, fg_ref[...]).astype(BF16)


def _ffn(x, g, weights, fg, *, layer, final, tm, tf, norm_next=False, shift=None):
    m, d = x.shape
    assert not (final and norm_next)
    emit = len(weights) == 2
    nf = D_FF // tf
    if emit:
        w_up, w_down = weights
        w_args = [w_up, w_up, w_down]
        w_specs = [
            pl.BlockSpec((None, d, tf), lambda i, j: (layer, 0, j)),
            pl.BlockSpec((None, d, tf), lambda i, j: (layer, 0, j + nf)),
            pl.BlockSpec((None, tf, d), lambda i, j: (layer, j, 0)),
        ]
    else:
        w_args = list(weights)
        w_specs = [None] * 3
    cast_specs = [
        pl.BlockSpec((d, tf), lambda i, j: (0, j)),
        pl.BlockSpec((d, tf), lambda i, j: (0, j)),
        pl.BlockSpec((tf, d), lambda i, j: (j, 0)),
    ]
    cast_shapes = [jax.ShapeDtypeStruct((d, D_FF), BF16)] * 2 + [jax.ShapeDtypeStruct((D_FF, d), BF16)]
    row_spec = pl.BlockSpec((tm, d), lambda i, j: (i, 0))
    extra_specs = (cast_specs if emit else []) + ([row_spec] if norm_next else [])
    extra_shapes = (cast_shapes if emit else []) + (
        [jax.ShapeDtypeStruct((m, d), BF16)] if norm_next else [])
    outs = _call(
        functools.partial(_ffn_kernel, nf=nf, final=final, emit=emit, norm_next=norm_next),
        name="ffn", grid=(m // tm, nf),
        in_specs=[
            pl.BlockSpec((tm, d), lambda i, j: (i, 0)),
            pl.BlockSpec((1, d), lambda i, j: (0, 0)),
            *(w_specs if emit else cast_specs),
            pl.BlockSpec((1, d), lambda i, j: (0, 0)),
        ],
        args=[x, g, *w_args, fg],
        out_specs=[row_spec] + extra_specs,
        out_shape=[jax.ShapeDtypeStruct((m, d), F32)] + extra_shapes,
        scratch=[pltpu.VMEM((tm, d), BF16)], sem=("arbitrary", "arbitrary"), shift=shift)
    n_cast = 3 if emit else 0
    return (outs[0], tuple(outs[1:4]) if emit else None,
            outs[1 + n_cast] if norm_next else None, outs[-1] if shift is not None else None)


def _rg_gates(xc, gaw, gab, gxw, gxb, sp):
    xcb = xc.astype(BF16)
    r = _sigmoid(jnp.dot(xcb, gaw, preferred_element_type=F32) + gab)
    ig = _sigmoid(jnp.dot(xcb, gxw, preferred_element_type=F32) + gxb)
    log_a = (-RG_C * r) * sp
    a = jnp.exp(log_a)
    b = jnp.sqrt(-jnp.tanh(log_a) * (1.0 + a * a)) * (ig * xc)
    return a, b


def _rg_prompt_kernel(x_ref, g_ref, win_ref, cw_ref, cb_ref, gaw_ref, gab_ref, gxw_ref, gxb_ref,
                      lam_ref, wout_ref, o_ref, hlast_ref, conv_ref,
                      ext_ref, gate_ref, y_ref, h_ref, *, tm, rider=None):
    i = pl.program_id(0)
    pad = SUBLANES

    @pl.when(i == 0)
    def _():
        ext_ref[0:pad, :] = jnp.zeros((pad, D_RNN), F32)
        h_ref[...] = jnp.zeros_like(h_ref)

    x = x_ref[...]
    y_ref[...] = _rms(x, g_ref[...]).astype(BF16)
    gate_ref[...] = jnp.dot(y_ref[...], win_ref[:, 0:D_RNN], preferred_element_type=F32)
    ext_ref[pad:pad + tm, :] = jnp.dot(y_ref[...], win_ref[:, D_RNN:2 * D_RNN],
                                       preferred_element_type=F32)

    sub = lax.broadcasted_iota(jnp.int32, (tm // SUBLANES, SUBLANES, RG_BLOCK), 1)
    for n in range(N_RG_BLOCKS):
        cs = slice(n * RG_BLOCK, (n + 1) * RG_BLOCK)
        _ride(rider, n, N_RG_BLOCKS)
        e = ext_ref[:, cs]
        taps = [pltpu.roll(e, CONV_WIDTH - 1 - j, 0)[pad:pad + tm, :]
                for j in range(CONV_WIDTH - 1)] + [e[pad:pad + tm, :]]
        xc = cb_ref[:, cs] + taps[0] * cw_ref[0:1, cs]
        for j in range(1, CONV_WIDTH):
            xc = xc + taps[j] * cw_ref[j:j + 1, cs]
        sp = _softplus(-lam_ref[:, cs])
        a, b = _rg_gates(xc, gaw_ref[n], gab_ref[:, cs], gxw_ref[n], gxb_ref[:, cs], sp)
        a = a.reshape(tm // SUBLANES, SUBLANES, RG_BLOCK)
        b = b.reshape(tm // SUBLANES, SUBLANES, RG_BLOCK)
        s = 1
        while s < SUBLANES:
            keep = sub >= s
            b = b + a * jnp.where(keep, pltpu.roll(b, s, 1), 0.0)
            a = a * jnp.where(keep, pltpu.roll(a, s, 1), 1.0)
            s *= 2
        carry = jnp.broadcast_to(h_ref[:, cs], (SUBLANES, RG_BLOCK))
        hs = []
        for k in range(tm // SUBLANES):
            hk = a[k] * carry + b[k]
            hs.append(hk)
            carry = jnp.broadcast_to(hk[SUBLANES - 1:SUBLANES, :], (SUBLANES, RG_BLOCK))
        h = jnp.concatenate(hs, axis=0)
        h_ref[:, cs] = carry[0:1, :]
        y_ref[:, cs] = (h * jax.nn.gelu(gate_ref[:, cs])).astype(BF16)

    o_ref[...] = x + jnp.dot(y_ref[...], wout_ref[...], preferred_element_type=F32)
    hlast_ref[...] = h_ref[...]
    conv_ref[...] = ext_ref[pad + tm - 3:pad + tm, :]
    ext_ref[0:pad, :] = ext_ref[tm:tm + pad, :]


def _rg_prompt(x, g, w_in, cw, cb, gaw, gab, gxw, gxb, lam, w_out, *, tm, shift=None):
    m, d = x.shape
    const2 = lambda i: (0, 0)
    const3 = lambda i: (0, 0, 0)
    one = pl.Buffered(1)
    return _call(
        functools.partial(_rg_prompt_kernel, tm=tm),
        name="rg_prompt", grid=(m // tm,),
        in_specs=[
            pl.BlockSpec((tm, d), lambda i: (i, 0)),
            pl.BlockSpec((1, d), const2),
            pl.BlockSpec((d, 2 * D_RNN), const2, pipeline_mode=one),
            pl.BlockSpec((CONV_WIDTH, D_RNN), const2),
            pl.BlockSpec((1, D_RNN), const2),
            pl.BlockSpec((N_RG_BLOCKS, RG_BLOCK, RG_BLOCK), const3),
            pl.BlockSpec((1, D_RNN), const2),
            pl.BlockSpec((N_RG_BLOCKS, RG_BLOCK, RG_BLOCK), const3),
            pl.BlockSpec((1, D_RNN), const2),
            pl.BlockSpec((1, D_RNN), const2),
            pl.BlockSpec((D_RNN, d), const2, pipeline_mode=one),
        ],
        args=[x, g, w_in, cw, cb, gaw, gab, gxw, gxb, lam, w_out],
        out_specs=[
            pl.BlockSpec((tm, d), lambda i: (i, 0)),
            pl.BlockSpec((1, D_RNN), const2),
            pl.BlockSpec((CONV_WIDTH - 1, D_RNN), const2),
        ],
        out_shape=[
            jax.ShapeDtypeStruct((m, d), F32),
            jax.ShapeDtypeStruct((1, D_RNN), F32),
            jax.ShapeDtypeStruct((CONV_WIDTH - 1, D_RNN), F32),
        ],
        scratch=[
            pltpu.VMEM((tm + SUBLANES, D_RNN), F32),
            pltpu.VMEM((tm, D_RNN), F32),
            pltpu.VMEM((tm, D_RNN), BF16),
            pltpu.VMEM((1, D_RNN), F32),
        ],
        sem="arbitrary", shift=shift)


def _rg_sample_kernel(x_ref, g_ref, win_ref, cw_ref, cb_ref, gaw_ref, gab_ref, gxw_ref, gxb_ref,
                      lam_ref, wout_ref, h0_ref, c0_ref, o_ref, hnew_ref, cnew_ref, y_ref):
    x = x_ref[...]
    xn = _rms(x, g_ref[...]).astype(BF16)
    gate = jnp.dot(xn, win_ref[:, 0:D_RNN], preferred_element_type=F32)
    rec = jnp.dot(xn, win_ref[:, D_RNN:2 * D_RNN], preferred_element_type=F32)
    for n in range(N_RG_BLOCKS):
        cs = slice(n * RG_BLOCK, (n + 1) * RG_BLOCK)
        taps = [c0_ref[:, j * D_RNN + n * RG_BLOCK:j * D_RNN + (n + 1) * RG_BLOCK]
                for j in range(CONV_WIDTH - 1)] + [rec[:, cs]]
        xc = cb_ref[:, cs] + taps[0] * cw_ref[0:1, cs]
        for j in range(1, CONV_WIDTH):
            xc = xc + taps[j] * cw_ref[j:j + 1, cs]
        sp = _softplus(-lam_ref[:, cs])
        a, b = _rg_gates(xc, gaw_ref[n], gab_ref[:, cs], gxw_ref[n], gxb_ref[:, cs], sp)
        h = a * h0_ref[:, cs] + b
        hnew_ref[:, cs] = h
        y_ref[:, cs] = (h * jax.nn.gelu(gate[:, cs])).astype(BF16)
    o_ref[...] = x + jnp.dot(y_ref[...], wout_ref[...], preferred_element_type=F32)
    cnew_ref[:, 0:2 * D_RNN] = c0_ref[:, D_RNN:3 * D_RNN]
    cnew_ref[:, 2 * D_RNN:3 * D_RNN] = rec


def _rg_sample(x, g, w_in, cw, cb, gaw, gab, gxw, gxb, lam, w_out, h0, c0):
    m, d = x.shape
    vm = pl.BlockSpec(memory_space=pltpu.VMEM)
    return pl.pallas_call(
        _rg_sample_kernel,
        in_specs=[vm] * 13,
        out_specs=[vm] * 3,
        out_shape=[
            jax.ShapeDtypeStruct((m, d), F32),
            jax.ShapeDtypeStruct((m, D_RNN), F32),
            jax.ShapeDtypeStruct((m, (CONV_WIDTH - 1) * D_RNN), F32),
        ],
        scratch_shapes=[pltpu.VMEM((m, D_RNN), BF16)],
        compiler_params=pltpu.CompilerParams(vmem_limit_bytes=VMEM_LIMIT_BYTES),
        name="rg_sample",
    )(x, g, w_in, cw, cb, gaw, gab, gxw, gxb, lam, w_out, h0, c0)


_PROJ_CHUNK = 512


def _rope_kernel(inv_ref, cos_ref, shi_ref, slo_ref, *, tr, pos0, pos_stride):
    half = ROPE_DIM // 2
    r = lax.broadcasted_iota(jnp.int32, (tr, LANES), 0)
    pos = pos0 + (pl.program_id(0) * tr + r) * pos_stride
    ang = pos.astype(F32) * inv_ref[...]
    lane = lax.broadcasted_iota(jnp.int32, (tr, LANES), 1)
    sin = jnp.sin(ang)
    cos_ref[...] = jnp.cos(ang)
    shi_ref[...] = jnp.where((lane >= half) & (lane < ROPE_DIM), sin, 0.0)
    slo_ref[...] = jnp.where(lane < half, -sin, 0.0)


def _rope_tables(inv_lane, n_rows, *, pos0, pos_stride):
    tr = min(n_rows, 1024)
    spec = pl.BlockSpec((tr, LANES), lambda i: (i, 0))
    return pl.pallas_call(
        functools.partial(_rope_kernel, tr=tr, pos0=pos0, pos_stride=pos_stride),
        grid=(n_rows // tr,),
        in_specs=[pl.BlockSpec((1, LANES), lambda i: (0, 0))],
        out_specs=[spec] * 3,
        out_shape=[jax.ShapeDtypeStruct((n_rows, LANES), F32)] * 3,
        compiler_params=_params("parallel"),
        name="rope_tables",
    )(inv_lane)


def _proj_kernel(*refs, tm, dil, n_mats, rope, scale, out_dtype, has_tail, rider=None):
    xn_ref, cos_ref, shi_ref, slo_ref = refs[:4]
    w_refs = refs[4:4 + n_mats]
    o_ref = refs[4 + n_mats]
    tail_ref = refs[5 + n_mats] if has_tail else None
    slab_ref = refs[-1]
    half = ROPE_DIM // 2
    th = tm

    for t0 in range(0, tm, th):
        rows = slice(t0, t0 + th)
        urows = slice(t0 // dil, (t0 + th) // dil)
        cos, s_hi, s_lo = cos_ref[rows, :], shi_ref[rows, :], slo_ref[rows, :]

        n_chunks = GW // _PROJ_CHUNK
        for k in range(n_mats):
            for c0 in range(0, GW, _PROJ_CHUNK):
                _ride(rider, k * n_chunks + c0 // _PROJ_CHUNK, n_mats * n_chunks)
                acc = jnp.dot(xn_ref[rows, :], w_refs[k][:, c0:c0 + _PROJ_CHUNK],
                              preferred_element_type=F32)
                for c in range(c0, c0 + _PROJ_CHUNK, LANES):
                    a = acc[:, c - c0:c - c0 + LANES]
                    if rope[k]:
                        a = (a * cos + pltpu.roll(a, half, 1) * s_hi
                             + pltpu.roll(a, LANES - half, 1) * s_lo)
                    if scale[k] != 1.0:
                        a = a * scale[k]
                    if has_tail:
                        tail_ref[k, rows, c:c + LANES] = a
                    if dil == 1:
                        o_ref[k, 0, rows, c:c + LANES] = a.astype(out_dtype)
                    else:
                        s = c // LANES % slab_ref.shape[0]
                        slab_ref[s, 0:th, :] = a
                        for r_ in range(dil):
                            o_ref[k, r_, urows, c:c + LANES] = (
                                slab_ref[s, pl.ds(r_, th // dil, stride=dil), :].astype(out_dtype))


def _proj(xn, tables, w, col_blocks, *, dil, rope, scale, tm, out_dtype, tail=0, shift=None):
    m, d = xn.shape
    n_mats = len(col_blocks)
    assert m % tm == 0 and tm % dil == 0
    lrows = m // dil
    tail_rows = max(tail, tm) if tail else 0
    i0 = (m - tail_rows) // tm if tail else 0
    in_specs = [pl.BlockSpec((tm, d), lambda i: (i, 0))] + [
        pl.BlockSpec((tm, LANES), lambda i: (i, 0))] * 3
    for cb in col_blocks:
        in_specs.append(pl.BlockSpec((d, GW), lambda i, cb=cb: (0, cb), pipeline_mode=pl.Buffered(1)))
    out_specs = [pl.BlockSpec((n_mats, dil, tm // dil, GW), lambda i: (0, 0, i, 0))]
    out_shape = [jax.ShapeDtypeStruct((n_mats, dil, lrows, GW), out_dtype)]
    if tail:
        out_specs.append(pl.BlockSpec((n_mats, tm, GW), lambda i: (0, jnp.maximum(i - i0, 0), 0)))
        out_shape.append(jax.ShapeDtypeStruct((n_mats, tail_rows, GW), F32))
    outs = _call(
        functools.partial(_proj_kernel, tm=tm, dil=dil, n_mats=n_mats, rope=tuple(rope),
                          scale=tuple(scale), out_dtype=out_dtype, has_tail=bool(tail)),
        name="proj", grid=(m // tm,), in_specs=in_specs, args=[xn, *tables] + [w] * n_mats,
        out_specs=out_specs, out_shape=out_shape, scratch=[pltpu.VMEM((2, tm, LANES), F32)],
        sem="arbitrary", shift=shift)
    return (outs[0], outs[1][:, tail_rows - tail:] if tail else None,
            outs[-1] if shift is not None else None)


_ATTN_HEADS_PER_ITER = 4


def _attn_kernel(q_ref, kc_ref, kp_ref, vc_ref, vp_ref, o_ref, lse_ref, *, tq, nb, rider=None):
    ub = pl.program_id(0) % nb
    blk = LANES
    nsub = tq // blk
    row = lax.broadcasted_iota(jnp.int32, (blk, blk), 0)
    col = lax.broadcasted_iota(jnp.int32, (blk, blk), 1)
    lane = lax.broadcasted_iota(jnp.int32, (blk, LANES), 1)
    cur_ok = col <= row
    prev_tri = col >= row
    neg = -jnp.inf
    dn = (((1,), (1,)), ((), ()))

    first_ok = jnp.logical_and(prev_tri, ub > 0)
    lse_ref[...] = jnp.zeros_like(lse_ref)

    rows = [slice(c * blk, (c + 1) * blk) for c in range(nsub)]
    units = [(e, c) for e in range(_ATTN_HEADS_PER_ITER) for c in range(nsub)]

    n_iter = N_HEADS // _ATTN_HEADS_PER_ITER

    def heads(it, carry):
        if rider is not None:
            rider.loop_part(it, n_iter)
        hn = [it * _ATTN_HEADS_PER_ITER + e for e in range(_ATTN_HEADS_PER_ITER)]
        hs = [pl.ds(pl.multiple_of(h * HEAD_DIM, HEAD_DIM), HEAD_DIM) for h in hn]
        ks = [[kp_ref[:, s]] + [kc_ref[rs, s] for rs in rows] for s in hs]
        vs = [[vp_ref[:, s]] + [vc_ref[rs, s] for rs in rows] for s in hs]
        scores = {}
        for e, c in units:
            q = q_ref[rows[c], hs[e]]
            prev_ok = prev_tri if c > 0 else first_ok
            s_p = jnp.where(prev_ok, lax.dot_general(q, ks[e][c], dn, preferred_element_type=F32), neg)
            s_c = jnp.where(cur_ok, lax.dot_general(q, ks[e][c + 1], dn, preferred_element_type=F32), neg)
            scores[e, c] = (s_p, s_c)
        probs = {}
        lse_new = [lse_ref[rs, :] for rs in rows]
        for e, c in units:
            s_p, s_c = scores[e, c]
            m = jnp.max(jnp.maximum(s_p, s_c), axis=1, keepdims=True)
            p_p = jnp.exp(s_p - m)
            p_c = jnp.exp(s_c - m)
            den = jnp.sum(p_p + p_c, axis=1, keepdims=True)
            lse_new[c] = jnp.where(lane == hn[e], m + jnp.log(den), lse_new[c])
            probs[e, c] = (p_p.astype(BF16), p_c.astype(BF16), 1.0 / den)
        for c, rs in enumerate(rows):
            lse_ref[rs, :] = lse_new[c]
        for e, c in units:
            p_p, p_c, rden = probs[e, c]
            o = (jnp.dot(p_p, vs[e][c], preferred_element_type=F32)
                 + jnp.dot(p_c, vs[e][c + 1], preferred_element_type=F32))
            o_ref[rows[c], hs[e]] = (o * rden).astype(o_ref.dtype)
        return carry

    lax.fori_loop(0, n_iter, heads, 0)
    _ride(rider, n_iter - 1, n_iter)


def _attn_group(q, kv, *, tq, shift=None):
    _, dil, lrows, _ = q.shape
    s = dil * lrows
    nb = lrows // tq
    sub = tq // LANES
    qv = q.reshape(s, GW)
    kvv = kv.reshape(2, s, GW)
    prev_idx = lambda i: jnp.maximum(i * sub - 1, 0)
    outs = _call(
        functools.partial(_attn_kernel, tq=tq, nb=nb),
        name="attn_group", grid=(s // tq,),
        in_specs=[
            pl.BlockSpec((tq, GW), lambda i: (i, 0)),
            pl.BlockSpec((None, tq, GW), lambda i: (0, i, 0)),
            pl.BlockSpec((None, LANES, GW), lambda i: (0, prev_idx(i), 0)),
            pl.BlockSpec((None, tq, GW), lambda i: (1, i, 0)),
            pl.BlockSpec((None, LANES, GW), lambda i: (1, prev_idx(i), 0)),
        ],
        args=[qv, kvv, kvv, kvv, kvv],
        out_specs=[
            pl.BlockSpec((tq, GW), lambda i: (i, 0)),
            pl.BlockSpec((tq, LANES), lambda i: (i, 0)),
        ],
        out_shape=[
            jax.ShapeDtypeStruct((s, GW), BF16),
            jax.ShapeDtypeStruct((s, LANES), F32),
        ],
        scratch=[], sem="arbitrary", shift=shift)
    return (outs[0].reshape(dil, lrows, GW), outs[1].reshape(dil, lrows, LANES),
            outs[-1] if shift is not None else None)


def _merge_wo_kernel(x_ref, o1_ref, o2_ref, o3_ref, l1_ref, l2_ref, l3_ref, w_ref, out_ref,
                     onat_ref, lnat_ref, a_ref, *, tm, rider=None):
    o_refs = (o1_ref, o2_ref, o3_ref)
    l_refs = (l1_ref, l2_ref, l3_ref)
    for g, dil in enumerate(DILATIONS):
        rows = tm // dil
        for r in range(dil):
            dst = pl.ds(r, rows, stride=dil) if dil > 1 else slice(None)
            lnat_ref[g, dst, :] = l_refs[g][r]
            for h in range(N_HEADS):
                onat_ref[g * N_HEADS + h, dst, :] = (
                    o_refs[g][r, :, h * HEAD_DIM:(h + 1) * HEAD_DIM].astype(F32))
    l1, l2, l3 = lnat_ref[0], lnat_ref[1], lnat_ref[2]
    mx = jnp.maximum(jnp.maximum(l1, l2), l3)
    e1, e2, e3 = jnp.exp(l1 - mx), jnp.exp(l2 - mx), jnp.exp(l3 - mx)
    tot = (e1 + e2) + e3
    w1, w2, w3 = e1 / tot, e2 / tot, e3 / tot
    for h in range(N_HEADS):
        _ride(rider, h, N_HEADS)
        a = onat_ref[h] * w1[:, h:h + 1]
        a = a + onat_ref[N_HEADS + h] * w2[:, h:h + 1]
        a = a + onat_ref[2 * N_HEADS + h] * w3[:, h:h + 1]
        a_ref[:, h * HEAD_DIM:(h + 1) * HEAD_DIM] = a.astype(BF16)
    out_ref[...] = x_ref[...] + jnp.dot(a_ref[...], w_ref[...], preferred_element_type=F32)


def _merge_wo(x, os_, ls_, w_o, *, tm, shift=None):
    m, d = x.shape
    rowblk = lambda i: (i, 0)
    deint = lambda i: (0, i, 0)
    in_specs = [pl.BlockSpec((tm, d), rowblk)]
    in_specs += [pl.BlockSpec((dil, tm // dil, GW), deint) for dil in DILATIONS]
    in_specs += [pl.BlockSpec((dil, tm // dil, LANES), deint) for dil in DILATIONS]
    in_specs += [pl.BlockSpec((GW, d), lambda i: (0, 0), pipeline_mode=pl.Buffered(1))]
    outs = _call(
        functools.partial(_merge_wo_kernel, tm=tm),
        name="merge_wo", grid=(m // tm,), in_specs=in_specs, args=[x, *os_, *ls_, w_o],
        out_specs=[pl.BlockSpec((tm, d), rowblk)],
        out_shape=[jax.ShapeDtypeStruct((m, d), F32)],
        scratch=[pltpu.VMEM((N_GROUPS * N_HEADS, tm, LANES), F32),
                 pltpu.VMEM((N_GROUPS, tm, LANES), F32),
                 pltpu.VMEM((tm, GW), BF16)],
        sem="arbitrary", shift=shift)
    return outs[0], (outs[-1] if shift is not None else None)


def _bf16_round(v):
    return v.astype(BF16).astype(F32)


def _attn_sample_kernel(q_ref, kvn_ref, c1_ref, c2_ref, c3_ref, o_ref):
    outs, lses = [], []
    for g, c_ref in enumerate((c1_ref, c2_ref, c3_ref)):
        q = _bf16_round(q_ref[g, 0])
        kn = _bf16_round(kvn_ref[g, 0, 0])
        vn = _bf16_round(kvn_ref[g, 1, 0])
        kc = _bf16_round(c_ref[0, :, 0, 0])
        vc = _bf16_round(c_ref[0, :, 0, 1])
        s_c = jnp.sum(kc * q[None], axis=-1, keepdims=True)
        s_n = jnp.sum(kn * q, axis=-1, keepdims=True)
        m = jnp.maximum(jnp.max(s_c, axis=0), s_n)
        p_c = jnp.exp(s_c - m[None])
        p_n = jnp.exp(s_n - m)
        den = jnp.sum(p_c, axis=0) + p_n
        o = jnp.sum(_bf16_round(p_c / den[None]) * vc, axis=0) + _bf16_round(p_n / den) * vn
        outs.append(o)
        lses.append(m + jnp.log(den))
    mx = jnp.maximum(jnp.maximum(lses[0], lses[1]), lses[2])
    es = [jnp.exp(l - mx) for l in lses]
    tot = (es[0] + es[1]) + es[2]
    acc = outs[0] * (es[0] / tot)
    acc = acc + outs[1] * (es[1] / tot)
    acc = acc + outs[2] * (es[2] / tot)
    o_ref[0] = acc


def _attn_sample(q, kvn, caches):
    b = q.shape[1]
    nk = N_KEYS - 1
    views = [c.reshape(b, nk, DILATIONS[g], 2, N_HEADS, HEAD_DIM) for g, c in enumerate(caches)]
    cspec = pl.BlockSpec((1, nk, 1, 2, N_HEADS, HEAD_DIM), lambda i: (i, 0, 0, 0, 0, 0))
    return pl.pallas_call(
        _attn_sample_kernel,
        grid=(b,),
        in_specs=[
            pl.BlockSpec((N_GROUPS, 1, N_HEADS, HEAD_DIM), lambda i: (0, i, 0, 0)),
            pl.BlockSpec((N_GROUPS, 2, 1, N_HEADS, HEAD_DIM), lambda i: (0, 0, i, 0, 0)),
            cspec, cspec, cspec,
        ],
        out_specs=pl.BlockSpec((1, N_HEADS, HEAD_DIM), lambda i: (i, 0, 0)),
        out_shape=jax.ShapeDtypeStruct((b, N_HEADS, HEAD_DIM), F32),
        compiler_params=_params("parallel"),
        name="attn_sample",
    )(q, kvn, *views)


def _matmul_res_kernel(x_ref, a_ref, w_ref, o_ref):
    o_ref[...] = x_ref[...] + jnp.dot(a_ref[...].astype(BF16), w_ref[...], preferred_element_type=F32)


def _matmul_res(x, a, w):
    vm = pl.BlockSpec(memory_space=pltpu.VMEM)
    return pl.pallas_call(
        _matmul_res_kernel,
        in_specs=[vm] * 3,
        out_specs=vm,
        out_shape=jax.ShapeDtypeStruct(x.shape, F32),
        compiler_params=pltpu.CompilerParams(vmem_limit_bytes=VMEM_LIMIT_BYTES),
        name="matmul_res",
    )(x, a, w)


def kernel(x_prompt, x_sample, state_h, state_conv, cache_kv_g1, cache_kv_g2, cache_kv_g3,
           ffn1_norm, ffn1_w_up, ffn1_w_down, mix_norm, ffn2_norm, ffn2_w_up, ffn2_w_down,
           rg_w_in, rg_conv_w, rg_conv_b, rg_gate_a_w, rg_gate_a_b, rg_gate_x_w, rg_gate_x_b,
           rg_lambda, rg_w_out, kv_norm, w_kv, attn_w_q, attn_w_o, final_norm):
    caches = [cache_kv_g1, cache_kv_g2, cache_kv_g3]
    bp, sp, d = x_prompt.shape
    db, ds, _ = x_sample.shape
    assert bp == 1 and ds == 1 and d == D_MODEL
    assert all(c.shape[1] == w for c, w in zip(caches, WINDOWS))
    past = PAST_LEN

    xp = x_prompt.reshape(sp, d)
    xs = x_sample.reshape(db, d)
    row = lambda v: v.reshape(1, -1)
    bf = lambda w: w.astype(BF16)

    half = ROPE_DIM // 2
    inv = ROPE_THETA ** (-jnp.arange(half, dtype=F32) / half)
    inv_lane = jnp.concatenate([inv, inv, jnp.zeros((LANES - ROPE_DIM,), F32)]).reshape(1, LANES)
    rope_p = _rope_tables(inv_lane, sp, pos0=0, pos_stride=1)
    rope_s = _rope_tables(inv_lane, db, pos0=past, pos_stride=0)
    fg = row(final_norm)

    def ffn_pair(xp, xs, g, w_up, w_down, layer, final=False, next_gain=None, shift=None):
        eg = fg if next_gain is None else row(next_gain)
        kw = dict(layer=layer, final=final, norm_next=next_gain is not None, tf=512)
        xs, w_bf, xns, _ = _ffn(xs, row(g[layer]), (w_up, w_down), eg, tm=db, **kw)
        xp, _, xnp, shifted = _ffn(xp, row(g[layer]), w_bf, eg, tm=512, shift=shift, **kw)
        return xp, xs, xnp, xns, shifted

    t12 = 128
    whole = lambda g: _ShiftJob(caches[g], None, 0, db * (WINDOWS[g] // t12), t12)
    t3 = 256
    nblk3 = db * (WINDOWS[2] // t3)
    shifted3, next_blk = None, 0

    def ride(steps):
        nonlocal next_blk
        count = min(steps, nblk3 - next_blk)
        job = _ShiftJob(caches[2], shifted3, next_blk, count, t3)
        next_blk += count
        return job

    xp, xs, _, _, shifted2 = ffn_pair(xp, xs, ffn1_norm, ffn1_w_up, ffn1_w_down, 0,
                                      shift=whole(1))
    rg = (row(mix_norm[0]), bf(rg_w_in[0]), rg_conv_w[0], row(rg_conv_b[0]), bf(rg_gate_a_w[0]),
          row(rg_gate_a_b[0]), bf(rg_gate_x_w[0]), row(rg_gate_x_b[0]), row(rg_lambda[0]),
          bf(rg_w_out[0]))
    xp, p_h, p_conv, shifted1 = _rg_prompt(xp, *rg, tm=256, shift=whole(0))
    xs, s_h, s_conv = _rg_sample(xs, *rg, state_h[0], state_conv[0].reshape(db, -1))
    xp, xs, xnp, xns, _ = ffn_pair(xp, xs, ffn2_norm, ffn2_w_up, ffn2_w_down, 0,
                                   next_gain=kv_norm)

    wkv = bf(w_kv)
    tm_p = 256
    kv_p, p_kv, kv_s = [], [], []
    for g in range(N_GROUPS):
        common = dict(rope=(True, False), scale=(1.0, 1.0))
        kv, tail, shifted3 = _proj(xnp, rope_p, wkv, (g, N_GROUPS + g), dil=DILATIONS[g],
                                   tm=tm_p, out_dtype=BF16, tail=WINDOWS[g],
                                   shift=ride(sp // tm_p), **common)
        kv_p.append(kv)
        p_kv.append(jnp.swapaxes(tail, 0, 1).reshape(1, WINDOWS[g], 2, N_HEADS, HEAD_DIM))
        kv_s.append(_proj(xns, rope_s, wkv, (g, N_GROUPS + g), dil=1, tm=db, out_dtype=F32,
                          **common)[0])
    kvn = jnp.stack(kv_s).reshape(N_GROUPS, 2, db, N_HEADS, HEAD_DIM)

    xp, xs, xnp, xns, _ = ffn_pair(xp, xs, ffn1_norm, ffn1_w_up, ffn1_w_down, 1,
                                   next_gain=mix_norm[1])
    wq = bf(attn_w_q[0])
    wo = bf(attn_w_o[0])
    qcommon = dict(rope=(True,), scale=(HEAD_DIM ** -0.5,))
    tq = 512
    og, lg, q_s = [], [], []
    for g in range(N_GROUPS):
        q, _, shifted3 = _proj(xnp, rope_p, wq, (g,), dil=DILATIONS[g], tm=tm_p, out_dtype=BF16, shift=ride(sp // tm_p), **qcommon)
        o, lse, shifted3 = _attn_group(q, kv_p[g], tq=tq, shift=ride(sp // tq))
        og.append(o)
        lg.append(lse)
        q_s.append(_proj(xns, rope_s, wq, (g,), dil=1, tm=db,
                         out_dtype=F32, **qcommon)[0])
    xp, shifted3 = _merge_wo(xp, og, lg, wo, tm=tm_p, shift=ride(sp // tm_p))
    assert next_blk == nblk3
    qs = jnp.stack(q_s).reshape(N_GROUPS, db, N_HEADS, HEAD_DIM)
    xs = _matmul_res(xs, _attn_sample(qs, kvn, caches).reshape(db, GW), wo)
    xp, xs, _, _, _ = ffn_pair(xp, xs, ffn2_norm, ffn2_w_up, ffn2_w_down, 1, final=True)

    s_kv = _insert_rows([kvn[g] for g in range(N_GROUPS)], [shifted1, shifted2, shifted3])

    return (xp.reshape(bp, sp, d), xs.reshape(db, ds, d),
            p_h.reshape(1, bp, D_RNN), p_conv.reshape(1, bp, CONV_WIDTH - 1, D_RNN),
            p_kv[0], p_kv[1], p_kv[2],
            s_h.reshape(1, db, D_RNN), s_conv.reshape(1, db, CONV_WIDTH - 1, D_RNN),
            s_kv[0], s_kv[1], s_kv[2])
```

```python
import functools
from typing import NamedTuple, Optional

import jax
import jax.numpy as jnp
from jax import lax
from jax.experimental import pallas as pl
from jax.experimental.pallas import tpu as pltpu

F32 = jnp.float32
BF16 = jnp.bfloat16

D_MODEL = 2048
D_FF = 5632
D_RNN = 2048
N_RG_BLOCKS = 16
RG_BLOCK = D_RNN // N_RG_BLOCKS
CONV_WIDTH = 4
RG_C = 8.0
HEAD_DIM = 128
N_HEADS = 16
N_GROUPS = 3
WINDOWS = (128, 512, 2048)
DILATIONS = (1, 4, 16)
N_KEYS = 129
ROPE_DIM = 32
ROPE_THETA = 500000.0
EPS = 1e-6
PAST_LEN = 8192
GW =N_HEADS * HEAD_DIM

LANES = 128
SUBLANES = 8
VMEM_LIMIT_BYTES = 60 * 1024 * 1024


def _params(*sem):
    return pltpu.CompilerParams(dimension_semantics=sem, vmem_limit_bytes=VMEM_LIMIT_BYTES)


def _rms(x, g):
    var = jnp.mean(x * x, axis=-1, keepdims=True)
    return (x * lax.rsqrt(var + EPS)) * g


def _softplus(z):
    return jnp.maximum(z, 0.0) + jnp.log1p(jnp.exp(-jnp.abs(z)))


def _sigmoid(z):
    return 0.5 * jnp.tanh(0.5 * z) + 0.5


class _ShiftJob(NamedTuple):
    cache: jax.Array
    prev: Optional[jax.Array]
    first: int
    count: int
    t: int


class _Rider:
    def __init__(self, c_ref, nxt_ref, o_ref):
        self.c, self.nxt, self.o = c_ref, nxt_ref, o_ref
        self.t = c_ref.shape[1]

    def part(self, p, n):
        rp = self.t // n
        lo, hi = p * rp, (p + 1) * rp
        if p == n - 1:
            self.o[0, lo:hi - 1] = self.c[0, lo + 1:hi]
            self.o[0, hi - 1] = self.nxt[0, 0]
        else:
            self.o[0, lo:hi] = self.c[0, lo + 1:hi + 1]

    def loop_part(self, it, n):
        rp = self.t // n
        lo = jnp.minimum(it, n - 2) * rp
        self.o[0, pl.ds(lo, rp)] = self.c[0, pl.ds(lo + 1, rp)]


def _ride(rider, p, n):
    if rider is not None:
        rider.part(p, n)


def _shift_blocks_per_entry(w, t):
    return -(-(w - 1) // t)


def _with_shift(kernel_fn, n_in, n_out, n_shift_in, job, grid):
    w, t = job.cache.shape[1], job.t
    nblk = _shift_blocks_per_entry(w, t)

    def wrapped(*refs):
        c_ref = refs[n_in]
        outs_at = n_in + n_shift_in
        o_ref = refs[outs_at + n_out]
        buf, sem_in, sem_out = refs[-3:]
        step = pl.program_id(0)
        for ax in range(1, len(grid)):
            step = step * grid[ax] + pl.program_id(ax)

        def copies(s):
            k = job.first + s
            bi = k // nblk
            src = jnp.minimum((k % nblk) * t + 1, w - t)
            slot = s % 2
            return (pltpu.make_async_copy(c_ref.at[bi, pl.ds(src, t)], buf.at[slot], sem_in.at[slot]),
                    pltpu.make_async_copy(buf.at[slot], o_ref.at[bi, pl.ds(src - 1, t)],
                                          sem_out.at[slot]))

        @pl.when(step == 0)
        def _():
            copies(step)[0].start()

        @pl.when(step < job.count)
        def _():
            cin, cout = copies(step)
            cin.wait()

            @pl.when(step >= 1)
            def _():
                copies(step - 1)[1].wait()

            @pl.when(step + 1 < job.count)
            def _():
                copies(step + 1)[0].start()

            cout.start()

            @pl.when(step == job.count - 1)
            def _():
                cout.wait()

        kernel_fn(*refs[:n_in], *refs[outs_at:outs_at + n_out], *refs[outs_at + n_out + 1:-3])
    return wrapped


def _call(kernel_fn, *, name, grid, in_specs, args, out_specs, out_shape, scratch, sem, shift=None):
    aliases = {}
    if shift is not None:
        assert shift.t < shift.cache.shape[1]
        any_spec = pl.BlockSpec(memory_space=pl.ANY)
        shift_in, shift_args = [any_spec], [shift.cache]
        if shift.prev is not None:
            shift_in.append(any_spec)
            shift_args.append(shift.prev)
            aliases = {len(in_specs) + 1: len(out_specs)}
        kernel_fn = _with_shift(kernel_fn, len(in_specs), len(out_specs), len(shift_in), shift, grid)
        in_specs = list(in_specs) + shift_in
        args = list(args) + shift_args
        out_specs = list(out_specs) + [any_spec]
        out_shape = list(out_shape) + [jax.ShapeDtypeStruct(shift.cache.shape, shift.cache.dtype)]
        scratch = list(scratch) + [pltpu.VMEM((2, shift.t) + shift.cache.shape[2:], shift.cache.dtype),
                                   pltpu.SemaphoreType.DMA((2,)), pltpu.SemaphoreType.DMA((2,))]
    sems = (sem,) if isinstance(sem, str) else sem
    return pl.pallas_call(
        kernel_fn, grid=grid, in_specs=in_specs, out_specs=out_specs, out_shape=out_shape,
        input_output_aliases=aliases, scratch_shapes=scratch, compiler_params=_params(*sems),
        name=name,
    )(*args)


def _insert_rows_kernel(*refs):
    n = len(refs) // 3
    for new_ref, o_ref in zip(refs[:n], refs[2 * n:]):
        o_ref[0, 0] = new_ref[:, 0]


def _insert_rows(kv_new, shifted):
    n = len(shifted)
    b = shifted[0].shape[0]
    tail = shifted[0].shape[2:]
    row_spec = lambda w: pl.BlockSpec((1, 1) + tail, lambda i: (i, w - 1, 0, 0, 0))
    return pl.pallas_call(
        _insert_rows_kernel,
        grid=(b,),
        in_specs=[pl.BlockSpec((2, 1) + tail[1:], lambda i: (0, i, 0, 0))] * n
        + [pl.BlockSpec(memory_space=pl.ANY)] * n,
        out_specs=[row_spec(s.shape[1]) for s in shifted],
        out_shape=[jax.ShapeDtypeStruct(s.shape, s.dtype) for s in shifted],
        input_output_aliases={n + k: k for k in range(n)},
        compiler_params=_params("arbitrary"),
        name="insert_rows",
    )(*kv_new, *shifted)


def _ffn_kernel(x_ref, g_ref, wg_ref, wu_ref, wd_ref, fg_ref, o_ref, *rest, nf, final, emit,
                norm_next, rider=None):
    xn_ref = rest[-1]
    j = pl.program_id(1)

    @pl.when(j == 0)
    def _():
        xn_ref[...] = _rms(x_ref[...], g_ref[...]).astype(BF16)
        o_ref[...] = jnp.zeros_like(o_ref)

    wg, wu, wd = wg_ref[...], wu_ref[...], wd_ref[...]
    if emit:
        wg, wu, wd = wg.astype(BF16), wu.astype(BF16), wd.astype(BF16)
        rest[0][...], rest[1][...], rest[2][...] = wg, wu, wd

    xn = xn_ref[...]
    _ride(rider, 0, 4)
    gate = jnp.dot(xn, wg, preferred_element_type=F32)
    _ride(rider, 1, 4)
    up = jnp.dot(xn, wu, preferred_element_type=F32)
    _ride(rider, 2, 4)
    h = (gate * jax.nn.sigmoid(gate)) * up
    _ride(rider, 3, 4)
    o_ref[...] += jnp.dot(h.astype(BF16), wd, preferred_element_type=F32)

    @pl.when(j == nf - 1)
    def _():
        y = x_ref[...] + 0.5 * o_ref[...]
        if final:
            y = _rms(y, fg_ref[...])
        o_ref[...] = y
        if norm_next:
            rest[-2][...] = _rms(y, fg_ref[...]).astype(BF16)


def _ffn(x, g, weights, fg, *, layer, final, tm, tf, norm_next=False, shift=None):
    m, d = x.shape
    assert not (final and norm_next)
    emit = len(weights) == 2
    nf = D_FF // tf
    if emit:
        w_up, w_down = weights
        w_args = [w_up, w_up, w_down]
        w_specs = [
            pl.BlockSpec((None, d, tf), lambda i, j: (layer, 0, j)),
            pl.BlockSpec((None, d, tf), lambda i, j: (layer, 0, j + nf)),
            pl.BlockSpec((None, tf, d), lambda i, j: (layer, j, 0)),
        ]
    else:
        w_args = list(weights)
        w_specs = [None] * 3
    cast_specs = [
        pl.BlockSpec((d, tf), lambda i, j: (0, j)),
        pl.BlockSpec((d, tf), lambda i, j: (0, j)),
        pl.BlockSpec((tf, d), lambda i, j: (j, 0)),
    ]
    cast_shapes = [jax.ShapeDtypeStruct((d, D_FF), BF16)] * 2 + [jax.ShapeDtypeStruct((D_FF, d), BF16)]
    row_spec = pl.BlockSpec((tm, d), lambda i, j: (i, 0))
    extra_specs = (cast_specs if emit else []) + ([row_spec] if norm_next else [])
    extra_shapes = (cast_shapes if emit else []) + (
        [jax.ShapeDtypeStruct((m, d), BF16)] if norm_next else [])
    outs = _call(
        functools.partial(_ffn_kernel, nf=nf, final=final, emit=emit, norm_next=norm_next),
        name="ffn", grid=(m // tm, nf),
        in_specs=[
            pl.BlockSpec((tm, d), lambda i, j: (i, 0)),
            pl.BlockSpec((1, d), lambda i, j: (0, 0)),
            *(w_specs if emit else cast_specs),
            pl.BlockSpec((1, d), lambda i, j: (0, 0)),
        ],
        args=[x, g, *w_args, fg],
        out_specs=[row_spec] + extra_specs,
        out_shape=[jax.ShapeDtypeStruct((m, d), F32)] + extra_shapes,
        scratch=[pltpu.VMEM((tm, d), BF16)], sem=("arbitrary", "arbitrary"), shift=shift)
    n_cast = 3 if emit else 0
    return (outs[0], tuple(outs[1:4]) if emit else None,
            outs[1 + n_cast] if norm_next else None, outs[-1] if shift is not None else None)


def _rg_gates(xc, gaw, gab, gxw, gxb, sp):
    xcb = xc.astype(BF16)
    r = _sigmoid(jnp.dot(xcb, gaw, preferred_element_type=F32) + gab)
    ig = _sigmoid(jnp.dot(xcb, gxw, preferred_element_type=F32) + gxb)
    log_a = (-RG_C * r) * sp
    a = jnp.exp(log_a)
    b = jnp.sqrt(-jnp.tanh(log_a) * (1.0 + a * a)) * (ig * xc)
    return a, b


def _rg_prompt_kernel(x_ref, g_ref, win_ref, cw_ref, cb_ref, gaw_ref, gab_ref, gxw_ref, gxb_ref,
                      lam_ref, wout_ref, o_ref, hlast_ref, conv_ref,
                      ext_ref, gate_ref, y_ref, h_ref, *, tm, rider=None):
    i = pl.program_id(0)
    pad = SUBLANES

    @pl.when(i == 0)
    def _():
        ext_ref[0:pad, :] = jnp.zeros((pad, D_RNN), F32)
        h_ref[...] = jnp.zeros_like(h_ref)

    x = x_ref[...]
    y_ref[...] = _rms(x, g_ref[...]).astype(BF16)
    gate_ref[...] = jnp.dot(y_ref[...], win_ref[:, 0:D_RNN], preferred_element_type=F32)
    ext_ref[pad:pad + tm, :] = jnp.dot(y_ref[...], win_ref[:, D_RNN:2 * D_RNN],
                                       preferred_element_type=F32)

    sub = lax.broadcasted_iota(jnp.int32, (tm // SUBLANES, SUBLANES, RG_BLOCK), 1)
    for n in range(N_RG_BLOCKS):
        cs = slice(n * RG_BLOCK, (n + 1) * RG_BLOCK)
        _ride(rider, n, N_RG_BLOCKS)
        e = ext_ref[:, cs]
        taps = [pltpu.roll(e, CONV_WIDTH - 1 - j, 0)[pad:pad + tm, :]
                for j in range(CONV_WIDTH - 1)] + [e[pad:pad + tm, :]]
        xc = cb_ref[:, cs] + taps[0] * cw_ref[0:1, cs]
        for j in range(1, CONV_WIDTH):
            xc = xc + taps[j] * cw_ref[j:j + 1, cs]
        sp = _softplus(-lam_ref[:, cs])
        a, b = _rg_gates(xc, gaw_ref[n], gab_ref[:, cs], gxw_ref[n], gxb_ref[:, cs], sp)
        a = a.reshape(tm // SUBLANES, SUBLANES, RG_BLOCK)
        b = b.reshape(tm // SUBLANES, SUBLANES, RG_BLOCK)
        s = 1
        while s < SUBLANES:
            keep = sub >= s
            b = b + a * jnp.where(keep, pltpu.roll(b, s, 1), 0.0)
            a = a * jnp.where(keep, pltpu.roll(a, s, 1), 1.0)
            s *= 2
        carry = jnp.broadcast_to(h_ref[:, cs], (SUBLANES, RG_BLOCK))
        hs = []
        for k in range(tm // SUBLANES):
            hk = a[k] * carry + b[k]
            hs.append(hk)
            carry = jnp.broadcast_to(hk[SUBLANES - 1:SUBLANES, :], (SUBLANES, RG_BLOCK))
        h = jnp.concatenate(hs, axis=0)
        h_ref[:, cs] = carry[0:1, :]
        y_ref[:, cs] = (h * jax.nn.gelu(gate_ref[:, cs])).astype(BF16)

    o_ref[...] = x + jnp.dot(y_ref[...], wout_ref[...], preferred_element_type=F32)
    hlast_ref[...] = h_ref[...]
    conv_ref[...] = ext_ref[pad + tm - 3:pad + tm, :]
    ext_ref[0:pad, :] = ext_ref[tm:tm + pad, :]


def _rg_prompt(x, g, w_in, cw, cb, gaw, gab, gxw, gxb, lam, w_out, *, tm, shift=None):
    m, d = x.shape
    const2 = lambda i: (0, 0)
    const3 = lambda i: (0, 0, 0)
    one = pl.Buffered(1)
    return _call(
        functools.partial(_rg_prompt_kernel, tm=tm),
        name="rg_prompt", grid=(m // tm,),
        in_specs=[
            pl.BlockSpec((tm, d), lambda i: (i, 0)),
            pl.BlockSpec((1, d), const2),
            pl.BlockSpec((d, 2 * D_RNN), const2, pipeline_mode=one),
            pl.BlockSpec((CONV_WIDTH, D_RNN), const2),
            pl.BlockSpec((1, D_RNN), const2),
            pl.BlockSpec((N_RG_BLOCKS, RG_BLOCK, RG_BLOCK), const3),
            pl.BlockSpec((1, D_RNN), const2),
            pl.BlockSpec((N_RG_BLOCKS, RG_BLOCK, RG_BLOCK), const3),
            pl.BlockSpec((1, D_RNN), const2),
            pl.BlockSpec((1, D_RNN), const2),
            pl.BlockSpec((D_RNN, d), const2, pipeline_mode=one),
        ],
        args=[x, g, w_in, cw, cb, gaw, gab, gxw, gxb, lam, w_out],
        out_specs=[
            pl.BlockSpec((tm, d), lambda i: (i, 0)),
            pl.BlockSpec((1, D_RNN), const2),
            pl.BlockSpec((CONV_WIDTH - 1, D_RNN), const2),
        ],
        out_shape=[
            jax.ShapeDtypeStruct((m, d), F32),
            jax.ShapeDtypeStruct((1, D_RNN), F32),
            jax.ShapeDtypeStruct((CONV_WIDTH - 1, D_RNN), F32),
        ],
        scratch=[
            pltpu.VMEM((tm + SUBLANES, D_RNN), F32),
            pltpu.VMEM((tm, D_RNN), F32),
            pltpu.VMEM((tm, D_RNN), BF16),
            pltpu.VMEM((1, D_RNN), F32),
        ],
        sem="arbitrary", shift=shift)


def _rg_sample_kernel(x_ref, g_ref, win_ref, cw_ref, cb_ref, gaw_ref, gab_ref, gxw_ref, gxb_ref,
                      lam_ref, wout_ref, h0_ref, c0_ref, o_ref, hnew_ref, cnew_ref, y_ref):
    x = x_ref[...]
    xn = _rms(x, g_ref[...]).astype(BF16)
    gate = jnp.dot(xn, win_ref[:, 0:D_RNN], preferred_element_type=F32)
    rec = jnp.dot(xn, win_ref[:, D_RNN:2 * D_RNN], preferred_element_type=F32)
    for n in range(N_RG_BLOCKS):
        cs = slice(n * RG_BLOCK, (n + 1) * RG_BLOCK)
        taps = [c0_ref[:, j * D_RNN + n * RG_BLOCK:j * D_RNN + (n + 1) * RG_BLOCK]
                for j in range(CONV_WIDTH - 1)] + [rec[:, cs]]
        xc = cb_ref[:, cs] + taps[0] * cw_ref[0:1, cs]
        for j in range(1, CONV_WIDTH):
            xc = xc + taps[j] * cw_ref[j:j + 1, cs]
        sp = _softplus(-lam_ref[:, cs])
        a, b = _rg_gates(xc, gaw_ref[n], gab_ref[:, cs], gxw_ref[n], gxb_ref[:, cs], sp)
        h = a * h0_ref[:, cs] + b
        hnew_ref[:, cs] = h
        y_ref[:, cs] = (h * jax.nn.gelu(gate[:, cs])).astype(BF16)
    o_ref[...] = x + jnp.dot(y_ref[...], wout_ref[...], preferred_element_type=F32)
    cnew_ref[:, 0:2 * D_RNN] = c0_ref[:, D_RNN:3 * D_RNN]
    cnew_ref[:, 2 * D_RNN:3 * D_RNN] = rec


def _rg_sample(x, g, w_in, cw, cb, gaw, gab, gxw, gxb, lam, w_out, h0, c0):
    m, d = x.shape
    vm = pl.BlockSpec(memory_space=pltpu.VMEM)
    return pl.pallas_call(
        _rg_sample_kernel,
        in_specs=[vm] * 13,
        out_specs=[vm] * 3,
        out_shape=[
            jax.ShapeDtypeStruct((m, d), F32),
            jax.ShapeDtypeStruct((m, D_RNN), F32),
            jax.ShapeDtypeStruct((m, (CONV_WIDTH - 1) * D_RNN), F32),
        ],
        scratch_shapes=[pltpu.VMEM((m, D_RNN), BF16)],
        compiler_params=pltpu.CompilerParams(vmem_limit_bytes=VMEM_LIMIT_BYTES),
        name="rg_sample",
    )(x, g, w_in, cw, cb, gaw, gab, gxw, gxb, lam, w_out, h0, c0)


_PROJ_CHUNK = 512


def _rope_kernel(inv_ref, cos_ref, shi_ref, slo_ref, *, tr, pos0, pos_stride):
    half = ROPE_DIM // 2
    r = lax.broadcasted_iota(jnp.int32, (tr, LANES), 0)
    pos = pos0 + (pl.program_id(0) * tr + r) * pos_stride
    ang = pos.astype(F32) * inv_ref[...]
    lane = lax.broadcasted_iota(jnp.int32, (tr, LANES), 1)
    sin = jnp.sin(ang)
    cos_ref[...] = jnp.cos(ang)
    shi_ref[...] = jnp.where((lane >= half) & (lane < ROPE_DIM), sin, 0.0)
    slo_ref[...] = jnp.where(lane < half, -sin, 0.0)


def _rope_tables(inv_lane, n_rows, *, pos0, pos_stride):
    tr = min(n_rows, 1024)
    spec = pl.BlockSpec((tr, LANES), lambda i: (i, 0))
    return pl.pallas_call(
        functools.partial(_rope_kernel, tr=tr, pos0=pos0, pos_stride=pos_stride),
        grid=(n_rows // tr,),
        in_specs=[pl.BlockSpec((1, LANES), lambda i: (0, 0))],
        out_specs=[spec] * 3,
        out_shape=[jax.ShapeDtypeStruct((n_rows, LANES), F32)] * 3,
        compiler_params=_params("parallel"),
        name="rope_tables",
    )(inv_lane)


def _proj_kernel(*refs, tm, dil, n_mats, rope, scale, out_dtype, has_tail, rider=None):
    xn_ref, cos_ref, shi_ref, slo_ref = refs[:4]
    w_refs = refs[4:4 + n_mats]
    o_ref = refs[4 + n_mats]
    tail_ref = refs[5 + n_mats] if has_tail else None
    slab_ref = refs[-1]
    half = ROPE_DIM // 2
    th = tm

    for t0 in range(0, tm, th):
        rows = slice(t0, t0 + th)
        urows = slice(t0 // dil, (t0 + th) // dil)
        cos, s_hi, s_lo = cos_ref[rows, :], shi_ref[rows, :], slo_ref[rows, :]

        n_chunks = GW // _PROJ_CHUNK
        for k in range(n_mats):
            for c0 in range(0, GW, _PROJ_CHUNK):
                _ride(rider, k * n_chunks + c0 // _PROJ_CHUNK, n_mats * n_chunks)
                acc = jnp.dot(xn_ref[rows, :], w_refs[k][:, c0:c0 + _PROJ_CHUNK],
                              preferred_element_type=F32)
                for c in range(c0, c0 + _PROJ_CHUNK, LANES):
                    a = acc[:, c - c0:c - c0 + LANES]
                    if rope[k]:
                        a = (a * cos + pltpu.roll(a, half, 1) * s_hi
                             + pltpu.roll(a, LANES - half, 1) * s_lo)
                    if scale[k] != 1.0:
                        a = a * scale[k]
                    if has_tail:
                        tail_ref[k, rows, c:c + LANES] = a
                    if dil == 1:
                        o_ref[k, 0, rows, c:c + LANES] = a.astype(out_dtype)
                    else:
                        s = c // LANES % slab_ref.shape[0]
                        slab_ref[s, 0:th, :] = a
                        for r_ in range(dil):
                            o_ref[k, r_, urows, c:c + LANES] = (
                                slab_ref[s, pl.ds(r_, th // dil, stride=dil), :].astype(out_dtype))


def _proj(xn, tables, w, col_blocks, *, dil, rope, scale, tm, out_dtype, tail=0, shift=None):
    m, d = xn.shape
    n_mats = len(col_blocks)
    assert m % tm == 0 and tm % dil == 0
    lrows = m // dil
    tail_rows = max(tail, tm) if tail else 0
    i0 = (m - tail_rows) // tm if tail else 0
    in_specs = [pl.BlockSpec((tm, d), lambda i: (i, 0))] + [
        pl.BlockSpec((tm, LANES), lambda i: (i, 0))] * 3
    for cb in col_blocks:
        in_specs.append(pl.BlockSpec((d, GW), lambda i, cb=cb: (0, cb), pipeline_mode=pl.Buffered(1)))
    out_specs = [pl.BlockSpec((n_mats, dil, tm // dil, GW), lambda i: (0, 0, i, 0))]
    out_shape = [jax.ShapeDtypeStruct((n_mats, dil, lrows, GW), out_dtype)]
    if tail:
        out_specs.append(pl.BlockSpec((n_mats, tm, GW), lambda i: (0, jnp.maximum(i - i0, 0), 0)))
        out_shape.append(jax.ShapeDtypeStruct((n_mats, tail_rows, GW), F32))
    outs = _call(
        functools.partial(_proj_kernel, tm=tm, dil=dil, n_mats=n_mats, rope=tuple(rope),
                          scale=tuple(scale), out_dtype=out_dtype, has_tail=bool(tail)),
        name="proj", grid=(m // tm,), in_specs=in_specs, args=[xn, *tables] + [w] * n_mats,
        out_specs=out_specs, out_shape=out_shape, scratch=[pltpu.VMEM((2, tm, LANES), F32)],
        sem="arbitrary", shift=shift)
    return (outs[0], outs[1][:, tail_rows - tail:] if tail else None,
            outs[-1] if shift is not None else None)


_ATTN_HEADS_PER_ITER = 4


def _attn_kernel(q_ref, kc_ref, kp_ref, vc_ref, vp_ref, o_ref, lse_ref, *, tq, nb, rider=None):
    ub = pl.program_id(0) % nb
    blk = LANES
    nsub = tq // blk
    row = lax.broadcasted_iota(jnp.int32, (blk, blk), 0)
    col = lax.broadcasted_iota(jnp.int32, (blk, blk), 1)
    lane = lax.broadcasted_iota(jnp.int32, (blk, LANES), 1)
    cur_ok = col <= row
    prev_tri = col >= row
    neg = -jnp.inf
    dn = (((1,), (1,)), ((), ()))

    first_ok = jnp.logical_and(prev_tri, ub > 0)
    lse_ref[...] = jnp.zeros_like(lse_ref)

    rows = [slice(c * blk, (c + 1) * blk) for c in range(nsub)]
    units = [(e, c) for e in range(_ATTN_HEADS_PER_ITER) for c in range(nsub)]

    n_iter = N_HEADS // _ATTN_HEADS_PER_ITER

    def heads(it, carry):
        if rider is not None:
            rider.loop_part(it, n_iter)
        hn = [it * _ATTN_HEADS_PER_ITER + e for e in range(_ATTN_HEADS_PER_ITER)]
        hs = [pl.ds(pl.multiple_of(h * HEAD_DIM, HEAD_DIM), HEAD_DIM) for h in hn]
        ks = [[kp_ref[:, s]] + [kc_ref[rs, s] for rs in rows] for s in hs]
        vs = [[vp_ref[:, s]] + [vc_ref[rs, s] for rs in rows] for s in hs]
        scores = {}
        for e, c in units:
            q = q_ref[rows[c], hs[e]]
            prev_ok = prev_tri if c > 0 else first_ok
            s_p = jnp.where(prev_ok, lax.dot_general(q, ks[e][c], dn, preferred_element_type=F32), neg)
            s_c = jnp.where(cur_ok, lax.dot_general(q, ks[e][c + 1], dn, preferred_element_type=F32), neg)
            scores[e, c] = (s_p, s_c)
        probs = {}
        lse_new = [lse_ref[rs, :] for rs in rows]
        for e, c in units:
            s_p, s_c = scores[e, c]
            m = jnp.max(jnp.maximum(s_p, s_c), axis=1, keepdims=True)
            p_p = jnp.exp(s_p - m)
            p_c = jnp.exp(s_c - m)
            den = jnp.sum(p_p + p_c, axis=1, keepdims=True)
            lse_new[c] = jnp.where(lane == hn[e], m + jnp.log(den), lse_new[c])
            probs[e, c] = (p_p.astype(BF16), p_c.astype(BF16), 1.0 / den)
        for c, rs in enumerate(rows):
            lse_ref[rs, :] = lse_new[c]
        for e, c in units:
            p_p, p_c, rden = probs[e, c]
            o = (jnp.dot(p_p, vs[e][c], preferred_element_type=F32)
                 + jnp.dot(p_c, vs[e][c + 1], preferred_element_type=F32))
            o_ref[rows[c], hs[e]] = (o * rden).astype(o_ref.dtype)
        return carry

    lax.fori_loop(0, n_iter, heads, 0)
    _ride(rider, n_iter - 1, n_iter)


def _attn_group(q, kv, *, tq, shift=None):
    _, dil, lrows, _ = q.shape
    s = dil * lrows
    nb = lrows // tq
    sub = tq // LANES
    qv = q.reshape(s, GW)
    kvv = kv.reshape(2, s, GW)
    prev_idx = lambda i: jnp.maximum(i * sub - 1, 0)
    outs = _call(
        functools.partial(_attn_kernel, tq=tq, nb=nb),
        name="attn_group", grid=(s // tq,),
        in_specs=[
            pl.BlockSpec((tq, GW), lambda i: (i, 0)),
            pl.BlockSpec((None, tq, GW), lambda i: (0, i, 0)),
            pl.BlockSpec((None, LANES, GW), lambda i: (0, prev_idx(i), 0)),
            pl.BlockSpec((None, tq, GW), lambda i: (1, i, 0)),
            pl.BlockSpec((None, LANES, GW), lambda i: (1, prev_idx(i), 0)),
        ],
        args=[qv, kvv, kvv, kvv, kvv],
        out_specs=[
            pl.BlockSpec((tq, GW), lambda i: (i, 0)),
            pl.BlockSpec((tq, LANES), lambda i: (i, 0)),
        ],
        out_shape=[
            jax.ShapeDtypeStruct((s, GW), BF16),
            jax.ShapeDtypeStruct((s, LANES), F32),
        ],
        scratch=[], sem="arbitrary", shift=shift)
    return (outs[0].reshape(dil, lrows, GW), outs[1].reshape(dil, lrows, LANES),
            outs[-1] if shift is not None else None)


def _merge_wo_kernel(x_ref, o1_ref, o2_ref, o3_ref, l1_ref, l2_ref, l3_ref, w_ref, out_ref,
                     onat_ref, lnat_ref, a_ref, *, tm, rider=None):
    o_refs = (o1_ref, o2_ref, o3_ref)
    l_refs = (l1_ref, l2_ref, l3_ref)
    for g, dil in enumerate(DILATIONS):
        rows = tm // dil
        for r in range(dil):
            dst = pl.ds(r, rows, stride=dil) if dil > 1 else slice(None)
            lnat_ref[g, dst, :] = l_refs[g][r]
            for h in range(N_HEADS):
                onat_ref[g * N_HEADS + h, dst, :] = (
                    o_refs[g][r, :, h * HEAD_DIM:(h + 1) * HEAD_DIM].astype(F32))
    l1, l2, l3 = lnat_ref[0], lnat_ref[1], lnat_ref[2]
    mx = jnp.maximum(jnp.maximum(l1, l2), l3)
    e1, e2, e3 = jnp.exp(l1 - mx), jnp.exp(l2 - mx), jnp.exp(l3 - mx)
    tot = (e1 + e2) + e3
    w1, w2, w3 = e1 / tot, e2 / tot, e3 / tot
    for h in range(N_HEADS):
        _ride(rider, h, N_HEADS)
        a = onat_ref[h] * w1[:, h:h + 1]
        a = a + onat_ref[N_HEADS + h] * w2[:, h:h + 1]
        a = a + onat_ref[2 * N_HEADS + h] * w3[:, h:h + 1]
        a_ref[:, h * HEAD_DIM:(h + 1) * HEAD_DIM] = a.astype(BF16)
    out_ref[...] = x_ref[...] + jnp.dot(a_ref[...], w_ref[...], preferred_element_type=F32)


def _merge_wo(x, os_, ls_, w_o, *, tm, shift=None):
    m, d = x.shape
    rowblk = lambda i: (i, 0)
    deint = lambda i: (0, i, 0)
    in_specs = [pl.BlockSpec((tm, d), rowblk)]
    in_specs += [pl.BlockSpec((dil, tm // dil, GW), deint) for dil in DILATIONS]
    in_specs += [pl.BlockSpec((dil, tm // dil, LANES), deint) for dil in DILATIONS]
    in_specs += [pl.BlockSpec((GW, d), lambda i: (0, 0), pipeline_mode=pl.Buffered(1))]
    outs = _call(
        functools.partial(_merge_wo_kernel, tm=tm),
        name="merge_wo", grid=(m // tm,), in_specs=in_specs, args=[x, *os_, *ls_, w_o],
        out_specs=[pl.BlockSpec((tm, d), rowblk)],
        out_shape=[jax.ShapeDtypeStruct((m, d), F32)],
        scratch=[pltpu.VMEM((N_GROUPS * N_HEADS, tm, LANES), F32),
                 pltpu.VMEM((N_GROUPS, tm, LANES), F32),
                 pltpu.VMEM((tm, GW), BF16)],
        sem="arbitrary", shift=shift)
    return outs[0], (outs[-1] if shift is not None else None)


def _bf16_round(v):
    return v.astype(BF16).astype(F32)


def _attn_sample_kernel(q_ref, kvn_ref, c1_ref, c2_ref, c3_ref, o_ref):
    outs, lses = [], []
    for g, c_ref in enumerate((c1_ref, c2_ref, c3_ref)):
        q = _bf16_round(q_ref[g, 0])
        kn = _bf16_round(kvn_ref[g, 0, 0])
        vn = _bf16_round(kvn_ref[g, 1, 0])
        kc = _bf16_round(c_ref[0, :, 0, 0])
        vc = _bf16_round(c_ref[0, :, 0, 1])
        s_c = jnp.sum(kc * q[None], axis=-1, keepdims=True)
        s_n = jnp.sum(kn * q, axis=-1, keepdims=True)
        m = jnp.maximum(jnp.max(s_c, axis=0), s_n)
        p_c = jnp.exp(s_c - m[None])
        p_n = jnp.exp(s_n - m)
        den = jnp.sum(p_c, axis=0) + p_n
        o = jnp.sum(_bf16_round(p_c / den[None]) * vc, axis=0) + _bf16_round(p_n / den) * vn
        outs.append(o)
        lses.append(m + jnp.log(den))
    mx = jnp.maximum(jnp.maximum(lses[0], lses[1]), lses[2])
    es = [jnp.exp(l - mx) for l in lses]
    tot = (es[0] + es[1]) + es[2]
    acc = outs[0] * (es[0] / tot)
    acc = acc + outs[1] * (es[1] / tot)
    acc = acc + outs[2] * (es[2] / tot)
    o_ref[0] = acc


def _attn_sample(q, kvn, caches):
    b = q.shape[1]
    nk = N_KEYS - 1
    views = [c.reshape(b, nk, DILATIONS[g], 2, N_HEADS, HEAD_DIM) for g, c in enumerate(caches)]
    cspec = pl.BlockSpec((1, nk, 1, 2, N_HEADS, HEAD_DIM), lambda i: (i, 0, 0, 0, 0, 0))
    return pl.pallas_call(
        _attn_sample_kernel,
        grid=(b,),
        in_specs=[
            pl.BlockSpec((N_GROUPS, 1, N_HEADS, HEAD_DIM), lambda i: (0, i, 0, 0)),
            pl.BlockSpec((N_GROUPS, 2, 1, N_HEADS, HEAD_DIM), lambda i: (0, 0, i, 0, 0)),
            cspec, cspec, cspec,
        ],
        out_specs=pl.BlockSpec((1, N_HEADS, HEAD_DIM), lambda i: (i, 0, 0)),
        out_shape=jax.ShapeDtypeStruct((b, N_HEADS, HEAD_DIM), F32),
        compiler_params=_params("parallel"),
        name="attn_sample",
    )(q, kvn, *views)


def _matmul_res_kernel(x_ref, a_ref, w_ref, o_ref):
    o_ref[...] = x_ref[...] + jnp.dot(a_ref[...].astype(BF16), w_ref[...], preferred_element_type=F32)


def _matmul_res(x, a, w):
    vm = pl.BlockSpec(memory_space=pltpu.VMEM)
    return pl.pallas_call(
        _matmul_res_kernel,
        in_specs=[vm] * 3,
        out_specs=vm,
        out_shape=jax.ShapeDtypeStruct(x.shape, F32),
        compiler_params=pltpu.CompilerParams(vmem_limit_bytes=VMEM_LIMIT_BYTES),
        name="matmul_res",
    )(x, a, w)


def kernel(x_prompt, x_sample, state_h, state_conv, cache_kv_g1, cache_kv_g2, cache_kv_g3,
           ffn1_norm, ffn1_w_up, ffn1_w_down, mix_norm, ffn2_norm, ffn2_w_up, ffn2_w_down,
           rg_w_in, rg_conv_w, rg_conv_b, rg_gate_a_w, rg_gate_a_b, rg_gate_x_w, rg_gate_x_b,
           rg_lambda, rg_w_out, kv_norm, w_kv, attn_w_q, attn_w_o, final_norm):
    caches = [cache_kv_g1, cache_kv_g2, cache_kv_g3]
    bp, sp, d = x_prompt.shape
    db, ds, _ = x_sample.shape
    assert bp == 1 and ds == 1 and d == D_MODEL
    assert all(c.shape[1] == w for c, w in zip(caches, WINDOWS))
    past = PAST_LEN

    xp = x_prompt.reshape(sp, d)
    xs = x_sample.reshape(db, d)
    row = lambda v: v.reshape(1, -1)
    bf = lambda w: w.astype(BF16)

    half = ROPE_DIM // 2
    inv = ROPE_THETA ** (-jnp.arange(half, dtype=F32) / half)
    inv_lane = jnp.concatenate([inv, inv, jnp.zeros((LANES - ROPE_DIM,), F32)]).reshape(1, LANES)
    rope_p = _rope_tables(inv_lane, sp, pos0=0, pos_stride=1)
    rope_s = _rope_tables(inv_lane, db, pos0=past, pos_stride=0)
    fg = row(final_norm)

    def ffn_pair(xp, xs, g, w_up, w_down, layer, final=False, next_gain=None, shift=None):
        eg = fg if next_gain is None else row(next_gain)
        kw = dict(layer=layer, final=final, norm_next=next_gain is not None, tf=512)
        xs, w_bf, xns, _ = _ffn(xs, row(g[layer]), (w_up, w_down), eg, tm=db, **kw)
        xp, _, xnp, shifted = _ffn(xp, row(g[layer]), w_bf, eg, tm=512, shift=shift, **kw)
        return xp, xs, xnp, xns, shifted

    t12 = (WINDOWS[0] - 1, 128)
    whole = lambda g: _ShiftJob(caches[g], None, 0,
                                db * _shift_blocks_per_entry(WINDOWS[g], t12[g]), t12[g])
    t3 = 256
    nblk3 = db * _shift_blocks_per_entry(WINDOWS[2], t3)
    shifted3, next_blk = None, 0

    def ride(steps):
        nonlocal next_blk
        count = min(steps, nblk3 - next_blk)
        job = _ShiftJob(caches[2], shifted3, next_blk, count, t3)
        next_blk += count
        return job

    xp, xs, _, _, shifted2 = ffn_pair(xp, xs, ffn1_norm, ffn1_w_up, ffn1_w_down, 0,
                                      shift=whole(1))
    rg = (row(mix_norm[0]), bf(rg_w_in[0]), rg_conv_w[0], row(rg_conv_b[0]), bf(rg_gate_a_w[0]),
          row(rg_gate_a_b[0]), bf(rg_gate_x_w[0]), row(rg_gate_x_b[0]), row(rg_lambda[0]),
          bf(rg_w_out[0]))
    xp, p_h, p_conv, shifted1 = _rg_prompt(xp, *rg, tm=256, shift=whole(0))
    xs, s_h, s_conv = _rg_sample(xs, *rg, state_h[0], state_conv[0].reshape(db, -1))
    xp, xs, xnp, xns, _ = ffn_pair(xp, xs, ffn2_norm, ffn2_w_up, ffn2_w_down, 0,
                                   next_gain=kv_norm)

    wkv = bf(w_kv)
    tm_p = 256
    kv_p, p_kv, kv_s = [], [], []
    for g in range(N_GROUPS):
        common = dict(rope=(True, False), scale=(1.0, 1.0))
        kv, tail, shifted3 = _proj(xnp, rope_p, wkv, (g, N_GROUPS + g), dil=DILATIONS[g],
                                   tm=tm_p, out_dtype=BF16, tail=WINDOWS[g],
                                   shift=ride(sp // tm_p), **common)
        kv_p.append(kv)
        p_kv.append(jnp.swapaxes(tail, 0, 1).reshape(1, WINDOWS[g], 2, N_HEADS, HEAD_DIM))
        kv_s.append(_proj(xns, rope_s, wkv, (g, N_GROUPS + g), dil=1, tm=db, out_dtype=F32,
                          **common)[0])
    kvn = jnp.stack(kv_s).reshape(N_GROUPS, 2, db, N_HEADS, HEAD_DIM)

    xp, xs, xnp, xns, _ = ffn_pair(xp, xs, ffn1_norm, ffn1_w_up, ffn1_w_down, 1,
                                   next_gain=mix_norm[1])
    wq = bf(attn_w_q[0])
    wo = bf(attn_w_o[0])
    qcommon = dict(rope=(True,), scale=(HEAD_DIM ** -0.5,))
    tq = 512
    og, lg, q_s = [], [], []
    for g in range(N_GROUPS):
        q, _, shifted3 = _proj(xnp, rope_p, wq, (g,), dil=DILATIONS[g], tm=tm_p, out_dtype=BF16, shift=ride(sp // tm_p), **qcommon)
        o, lse, shifted3 = _attn_group(q, kv_p[g], tq=tq, shift=ride(sp // tq))
        og.append(o)
        lg.append(lse)
        q_s.append(_proj(xns, rope_s, wq, (g,), dil=1, tm=db,
                         out_dtype=F32, **qcommon)[0])
    xp, shifted3 = _merge_wo(xp, og, lg, wo, tm=tm_p, shift=ride(sp // tm_p))
    assert next_blk == nblk3
    qs = jnp.stack(q_s).reshape(N_GROUPS, db, N_HEADS, HEAD_DIM)
    xs = _matmul_res(xs, _attn_sample(qs, kvn, caches).reshape(db, GW), wo)
    xp, xs, _, _, _ = ffn_pair(xp, xs, ffn2_norm, ffn2_w_up, ffn2_w_down, 1, final=True)

    s_kv = _insert_rows([kvn[g] for g in range(N_GROUPS)], [shifted1, shifted2, shifted3])

    return (xp.reshape(bp, sp, d), xs.reshape(db, ds, d),
            p_h.reshape(1, bp, D_RNN), p_conv.reshape(1, bp, CONV_WIDTH - 1, D_RNN),
            p_kv[0], p_kv[1], p_kv[2],
            s_h.reshape(1, db, D_RNN), s_conv.reshape(1, db, CONV_WIDTH - 1, D_RNN),
            s_kv[0], s_kv[1], s_kv[2])
```
